```python
import jax, jax.numpy as jnp
from jax import lax
import numpy as np

D_MODEL = 1024
BATCH = 8
SEQ = 4096
DEPTH = 1

D_RNN = D_MODEL
RNN_HEADS = 8
RNN_HEAD_DIM = D_RNN // RNN_HEADS
CONV_WIDTH = 4
LRU_C = 8.0
D_GMLP = D_MODEL
GMLP_HEADS = 8
GMLP_HEAD_DIM = D_GMLP // GMLP_HEADS
CHUNK = 128
N_BRANCH = 2
D_IN = 2 * D_RNN + 2 * D_GMLP + N_BRANCH * D_MODEL
PEER_HEADS = 8
N_KEYS = 128
N_EXPERTS = N_KEYS * N_KEYS
D_KEY = 256
D_HALF = D_KEY // 2
TOPK = 16
TOK_BLOCK = 128
ALPHA = (2.0 * DEPTH) ** 0.25
BETA = (8.0 * DEPTH) ** -0.25
LN_EPS = 1e-5

kernel_name = "hybrid_rglru_gmlp_peer_deepnorm"


def layer_norm(x, g, b):
    xf = x.astype(jnp.float32)
    mu = jnp.mean(xf, axis=-1, keepdims=True)
    var = jnp.mean(jnp.square(xf - mu), axis=-1, keepdims=True)
    y = (xf - mu) * lax.rsqrt(var + LN_EPS) * g.astype(jnp.float32) + b.astype(jnp.float32)
    return y.astype(x.dtype)


def causal_depthwise_conv(x, w, b):
    s = x.shape[1]
    xp = jnp.pad(x, ((0, 0), (CONV_WIDTH - 1, 0), (0, 0)))
    return sum(w[k] * xp[:, k:k + s] for k in range(CONV_WIDTH)) + b


def rg_lru(x, w_a, b_a, w_x, b_x, lam):
    bsz, s, _ = x.shape
    xh = x.reshape(bsz, s, RNN_HEADS, RNN_HEAD_DIM)
    r = jax.nn.sigmoid(jnp.einsum('bshi,hij->bshj', xh, w_a).reshape(bsz, s, D_RNN) + b_a)
    i = jax.nn.sigmoid(jnp.einsum('bshi,hij->bshj', xh, w_x).reshape(bsz, s, D_RNN) + b_x)
    log_a = -LRU_C * r.astype(jnp.float32) * jax.nn.softplus(-lam.astype(jnp.float32))
    a = jnp.exp(log_a)
    mult = jnp.sqrt(-jnp.expm1(2.0 * log_a))
    bx = mult * (i * x).astype(jnp.float32)

    def combine(e1, e2):
        a1, b1 = e1
        a2, b2 = e2
        return a1 * a2, a2 * b1 + b2

    _, h = lax.associative_scan(combine, (a, bx), axis=1)
    return h.astype(x.dtype)


def chunked_spatial_gating(u, v, ln_g, ln_b, ws, bs):
    bsz, s, _ = u.shape
    u = jax.nn.gelu(u)
    v = layer_norm(jax.nn.gelu(v), ln_g, ln_b)
    vc = v.reshape(bsz, s // CHUNK, CHUNK, GMLP_HEADS, GMLP_HEAD_DIM)
    mask = jnp.tril(jnp.ones((CHUNK, CHUNK), dtype=bool))
    wsm = jnp.where(mask[None], ws, jnp.zeros_like(ws))
    mixed = jnp.einsum('hts,bcshd->bcthd', wsm, vc) + bs.T[None, None, :, :, None]
    return u * mixed.reshape(bsz, s, D_GMLP)


def peer(x, wq, sub_keys, u_tab, v_tab):
    bsz, s, d = x.shape
    t = bsz * s
    xt = x.reshape(t, d)
    q = (xt @ wq).reshape(t, PEER_HEADS, 2, D_HALF).astype(jnp.float32)
    scores = jnp.einsum('thpd,hpnd->thpn', q, sub_keys.astype(jnp.float32))
    s1, i1 = lax.top_k(scores[:, :, 0], TOPK)
    s2, i2 = lax.top_k(scores[:, :, 1], TOPK)
    cand_s = (s1[..., :, None] + s2[..., None, :]).reshape(t, PEER_HEADS, TOPK * TOPK)
    cand_i = (i1[..., :, None] * N_KEYS + i2[..., None, :]).reshape(t, PEER_HEADS, TOPK * TOPK)
    top_s, pos = lax.top_k(cand_s, TOPK)
    idx = jnp.take_along_axis(cand_i, pos, axis=-1)
    gate = jax.nn.softmax(top_s, axis=-1).astype(x.dtype)
    nb = t // TOK_BLOCK

    def block(args):
        xb, ib, gb = args
        u_sel = u_tab[ib]
        h = jnp.einsum('thkd,td->thk', u_sel, xb)
        act = jax.nn.gelu(h) * gb
        return jnp.einsum('thk,thkd->td', act, v_tab[ib])

    out = lax.map(block, (xt.reshape(nb, TOK_BLOCK, d),
                          idx.reshape(nb, TOK_BLOCK, PEER_HEADS, TOPK),
                          gate.reshape(nb, TOK_BLOCK, PEER_HEADS, TOPK)))
    return out.reshape(bsz, s, d)


def setup_inputs(seed: int = 0) -> dict:
    key = jax.random.key(seed)
    ks = jax.random.split(key, 24)
    nrm = lambda k, shape, scale: jax.random.normal(k, shape, jnp.float32) * scale
    L = DEPTH
    a0 = jax.random.uniform(ks[9], (L, D_RNN), jnp.float32, minval=0.9, maxval=0.999)
    return {
        "x": nrm(ks[0], (BATCH, SEQ, D_MODEL), 1.0),
        "w_in": nrm(ks[1], (L, D_MODEL, D_IN), D_MODEL ** -0.5),
        "b_gate": nrm(ks[2], (L, N_BRANCH, D_MODEL), 0.02),
        "conv_w": nrm(ks[3], (L, CONV_WIDTH, D_RNN), CONV_WIDTH ** -0.5),
        "conv_b": nrm(ks[4], (L, D_RNN), 0.02),
        "rg_w_a": nrm(ks[5], (L, RNN_HEADS, RNN_HEAD_DIM, RNN_HEAD_DIM), RNN_HEAD_DIM ** -0.5),
        "rg_b_a": nrm(ks[6], (L, D_RNN), 0.02),
        "rg_w_x": nrm(ks[7], (L, RNN_HEADS, RNN_HEAD_DIM, RNN_HEAD_DIM), RNN_HEAD_DIM ** -0.5),
        "rg_b_x": nrm(ks[8], (L, D_RNN), 0.02),
        "rg_lambda": jnp.log(a0) - jnp.log1p(-a0),
        "gm_ln_g": 1.0 + nrm(ks[10], (L, D_GMLP), 0.02),
        "gm_ln_b": nrm(ks[11], (L, D_GMLP), 0.02),
        "gm_ws": nrm(ks[12], (L, GMLP_HEADS, CHUNK, CHUNK), CHUNK ** -0.5),
        "gm_bs": 1.0 + nrm(ks[13], (L, GMLP_HEADS, CHUNK), 0.02),
        "w_out": nrm(ks[14], (L, D_MODEL, D_MODEL), BETA * D_MODEL ** -0.5),
        "ln1_g": 1.0 + nrm(ks[15], (L, D_MODEL), 0.02),
        "ln1_b": nrm(ks[16], (L, D_MODEL), 0.02),
        "peer_wq": nrm(ks[17], (L, D_MODEL, PEER_HEADS * D_KEY), D_MODEL ** -0.5),
        "peer_sub_keys": nrm(ks[18], (L, PEER_HEADS, 2, N_KEYS, D_HALF), D_HALF ** -0.5),
        "peer_u": nrm(ks[19], (L, N_EXPERTS, D_MODEL), D_MODEL ** -0.5),
        "peer_v": nrm(ks[20], (L, N_EXPERTS, D_MODEL), BETA),
        "ln2_g": 1.0 + nrm(ks[21], (L, D_MODEL), 0.02),
        "ln2_b": nrm(ks[22], (L, D_MODEL), 0.02),
    }


def reference(x, w_in, b_gate, conv_w, conv_b, rg_w_a, rg_b_a, rg_w_x, rg_b_x, rg_lambda,
              gm_ln_g, gm_ln_b, gm_ws, gm_bs, w_out, ln1_g, ln1_b,
              peer_wq, peer_sub_keys, peer_u, peer_v, ln2_g, ln2_b):
    bsz, s, _ = x.shape
    splits = [D_RNN, 2 * D_RNN, 2 * D_RNN + D_GMLP, 2 * D_RNN + 2 * D_GMLP]
    for l in range(DEPTH):
        proj = x @ w_in[l]
        xa, ga, u, v, g = jnp.split(proj, splits, axis=-1)
        ya = jax.nn.gelu(ga) * rg_lru(causal_depthwise_conv(xa, conv_w[l], conv_b[l]),
                                      rg_w_a[l], rg_b_a[l], rg_w_x[l], rg_b_x[l], rg_lambda[l])
        yb = chunked_spatial_gating(u, v, gm_ln_g[l], gm_ln_b[l], gm_ws[l], gm_bs[l])
        gates = jax.nn.sigmoid(g.reshape(bsz, s, N_BRANCH, D_MODEL) + b_gate[l])
        y = gates[:, :, 0] * ya + gates[:, :, 1] * yb
        x = layer_norm(ALPHA * x + y @ w_out[l], ln1_g[l], ln1_b[l])
        x = layer_norm(ALPHA * x + peer(x, peer_wq[l], peer_sub_keys[l], peer_u[l], peer_v[l]),
                       ln2_g[l], ln2_b[l])
    return x
```

```python
import functools

import jax
import jax.numpy as jnp
from jax import lax
from jax.experimental import pallas as pl
from jax.experimental.pallas import tpu as pltpu

D_MODEL = 1024
RNN_HEADS = 8
RNN_HEAD_DIM = D_MODEL // RNN_HEADS
CONV_WIDTH = 4
LRU_C = 8.0
GMLP_HEADS = 8
GMLP_HEAD_DIM = D_MODEL // GMLP_HEADS
CHUNK = 128
N_BRANCH = 2
PEER_HEADS = 8
N_KEYS = 128
D_HALF = 128
TOPK = 16
TOK_BLOCK = 128
ALPHA = 2.0 ** 0.25
LN_EPS = 1e-5


def _mm_kernel(x_ref, w_ref, o_ref):
    o_ref[...] = jnp.dot(x_ref[...].astype(jnp.bfloat16), w_ref[...],
                         preferred_element_type=jnp.float32)


def _matmul(x, w_bf16, tm=512, tn=1024):
    m, k = x.shape
    n = w_bf16.shape[1]
    return pl.pallas_call(
        _mm_kernel,
        grid=(m // tm, n // tn),
        in_specs=[pl.BlockSpec((tm, k), lambda i, j: (i, 0)),
                  pl.BlockSpec((k, tn), lambda i, j: (0, j))],
        out_specs=pl.BlockSpec((tm, tn), lambda i, j: (i, j)),
        out_shape=jax.ShapeDtypeStruct((m, n), jnp.float32),
        name="matmul",
    )(x, w_bf16)


def _layer_norm(x, g, b):
    mu = jnp.mean(x, axis=-1, keepdims=True)
    var = jnp.mean(jnp.square(x - mu), axis=-1, keepdims=True)
    return (x - mu) * lax.rsqrt(var + LN_EPS) * g + b


def _conv(x, w, b):
    s = x.shape[1]
    xp = jnp.pad(x, ((0, 0), (CONV_WIDTH - 1, 0), (0, 0)))
    return sum(w[k] * xp[:, k:k + s] for k in range(CONV_WIDTH)) + b


def _rg_lru(x, w_a, b_a, w_x, b_x, lam):
    bsz, s, _ = x.shape
    xh = x.reshape(bsz, s, RNN_HEADS, RNN_HEAD_DIM)
    r = jax.nn.sigmoid(jnp.einsum('bshi,hij->bshj', xh, w_a).reshape(bsz, s, D_MODEL) + b_a)
    i = jax.nn.sigmoid(jnp.einsum('bshi,hij->bshj', xh, w_x).reshape(bsz, s, D_MODEL) + b_x)
    log_a = -LRU_C * r * jax.nn.softplus(-lam)
    a = jnp.exp(log_a)
    mult = jnp.sqrt(-jnp.expm1(2.0 * log_a))
    bx = mult * (i * x)

    def combine(e1, e2):
        a1, b1 = e1
        a2, b2 = e2
        return a1 * a2, a2 * b1 + b2

    _, h = lax.associative_scan(combine, (a, bx), axis=1)
    return h


def _gating(u, v, ln_g, ln_b, ws, bs):
    bsz, s, _ = u.shape
    u = jax.nn.gelu(u)
    v = _layer_norm(jax.nn.gelu(v), ln_g, ln_b)
    vc = v.reshape(bsz, s // CHUNK, CHUNK, GMLP_HEADS, GMLP_HEAD_DIM)
    mask = jnp.tril(jnp.ones((CHUNK, CHUNK), dtype=bool))
    wsm = jnp.where(mask[None], ws, jnp.zeros_like(ws))
    mixed = jnp.einsum('hts,bcshd->bcthd', wsm, vc) + bs.T[None, None, :, :, None]
    return u * mixed.reshape(bsz, s, D_MODEL)


def _peer(xt, wq, sub_keys, u_tab, v_tab):
    t, d = xt.shape
    q = _matmul(xt, wq.astype(jnp.bfloat16)).reshape(t, PEER_HEADS, 2, D_HALF)
    scores = jnp.einsum('thpd,hpnd->thpn', q, sub_keys)
    s1, i1 = lax.top_k(scores[:, :, 0], TOPK)
    s2, i2 = lax.top_k(scores[:, :, 1], TOPK)
    cand_s = (s1[..., :, None] + s2[..., None, :]).reshape(t, PEER_HEADS, TOPK * TOPK)
    cand_i = (i1[..., :, None] * N_KEYS + i2[..., None, :]).reshape(t, PEER_HEADS, TOPK * TOPK)
    top_s, pos = lax.top_k(cand_s, TOPK)
    idx = jnp.take_along_axis(cand_i, pos, axis=-1)
    gate = jax.nn.softmax(top_s, axis=-1)
    nb = t // TOK_BLOCK

    def block(args):
        xb, ib, gb = args
        u_sel = u_tab[ib]
        h = jnp.einsum('thkd,td->thk', u_sel, xb)
        act = jax.nn.gelu(h) * gb
        return jnp.einsum('thk,thkd->td', act, v_tab[ib])

    out = lax.map(block, (xt.reshape(nb, TOK_BLOCK, d),
                          idx.reshape(nb, TOK_BLOCK, PEER_HEADS, TOPK),
                          gate.reshape(nb, TOK_BLOCK, PEER_HEADS, TOPK)))
    return out.reshape(t, d)


def kernel(x, w_in, b_gate, conv_w, conv_b, rg_w_a, rg_b_a, rg_w_x, rg_b_x, rg_lambda, gm_ln_g, gm_ln_b, gm_ws, gm_bs, w_out, ln1_g, ln1_b, peer_wq, peer_sub_keys, peer_u, peer_v, ln2_g, ln2_b):
    bsz, s, d = x.shape
    t = bsz * s
    l = 0
    xt = x.reshape(t, d)
    proj = _matmul(xt, w_in[l].astype(jnp.bfloat16)).reshape(bsz, s, -1)
    xa, ga, u, v, g = jnp.split(proj, [d, 2 * d, 3 * d, 4 * d], axis=-1)
    ya = jax.nn.gelu(ga) * _rg_lru(_conv(xa, conv_w[l], conv_b[l]),
                                   rg_w_a[l], rg_b_a[l], rg_w_x[l], rg_b_x[l], rg_lambda[l])
    yb = _gating(u, v, gm_ln_g[l], gm_ln_b[l], gm_ws[l], gm_bs[l])
    gates = jax.nn.sigmoid(g.reshape(bsz, s, N_BRANCH, d) + b_gate[l])
    y = gates[:, :, 0] * ya + gates[:, :, 1] * yb
    o = _matmul(y.reshape(t, d), w_out[l].astype(jnp.bfloat16))
    x1 = _layer_norm(ALPHA * xt + o, ln1_g[l], ln1_b[l])
    p = _peer(x1, peer_wq[l], peer_sub_keys[l], peer_u[l], peer_v[l])
    x2 = _layer_norm(ALPHA * x1 + p, ln2_g[l], ln2_b[l])
    return x2.reshape(bsz, s, d)
```

```python
import jax
import jax.numpy as jnp
from jax import lax
from jax.experimental import pallas as pl
from jax.experimental.pallas import tpu as pltpu

D_MODEL = 1024
RNN_HEADS = 8
RNN_HEAD_DIM = D_MODEL // RNN_HEADS
CONV_WIDTH = 4
LRU_C = 8.0
GMLP_HEADS = 8
GMLP_HEAD_DIM = D_MODEL // GMLP_HEADS
CHUNK = 128
N_BRANCH = 2
PEER_HEADS = 8
N_KEYS = 128
D_HALF = 128
TOPK = 16
ALPHA = 2.0 ** 0.25
LN_EPS = 1e-5
LANES = 128
D_TILES = D_MODEL // LANES


def _mm_kernel(x_ref, w_ref, o_ref):
    o_ref[...] = jnp.dot(x_ref[...].astype(jnp.bfloat16), w_ref[...],
                         preferred_element_type=jnp.float32)


def _matmul(x, w_bf16, tm=512, tn=1024):
    m, k = x.shape
    n = w_bf16.shape[1]
    return pl.pallas_call(
        _mm_kernel,
        grid=(m // tm, n // tn),
        in_specs=[pl.BlockSpec((tm, k), lambda i, j: (i, 0)),
                  pl.BlockSpec((k, tn), lambda i, j: (0, j))],
        out_specs=pl.BlockSpec((tm, tn), lambda i, j: (i, j)),
        out_shape=jax.ShapeDtypeStruct((m, n), jnp.float32),
        name="matmul",
    )(x, w_bf16)


def _layer_norm(x, g, b):
    mu = jnp.mean(x, axis=-1, keepdims=True)
    var = jnp.mean(jnp.square(x - mu), axis=-1, keepdims=True)
    return (x - mu) * lax.rsqrt(var + LN_EPS) * g + b


def _conv(x, w, b):
    s = x.shape[1]
    xp = jnp.pad(x, ((0, 0), (CONV_WIDTH - 1, 0), (0, 0)))
    return sum(w[k] * xp[:, k:k + s] for k in range(CONV_WIDTH)) + b


def _rg_lru(x, w_a, b_a, w_x, b_x, lam):
    bsz, s, _ = x.shape
    xh = x.reshape(bsz, s, RNN_HEADS, RNN_HEAD_DIM)
    r = jax.nn.sigmoid(jnp.einsum('bshi,hij->bshj', xh, w_a).reshape(bsz, s, D_MODEL) + b_a)
    i = jax.nn.sigmoid(jnp.einsum('bshi,hij->bshj', xh, w_x).reshape(bsz, s, D_MODEL) + b_x)
    log_a = -LRU_C * r * jax.nn.softplus(-lam)
    a = jnp.exp(log_a)
    mult = jnp.sqrt(-jnp.expm1(2.0 * log_a))
    bx = mult * (i * x)

    def combine(e1, e2):
        a1, b1 = e1
        a2, b2 = e2
        return a1 * a2, a2 * b1 + b2

    _, h = lax.associative_scan(combine, (a, bx), axis=1)
    return h


def _gating(u, v, ln_g, ln_b, ws, bs):
    bsz, s, _ = u.shape
    u = jax.nn.gelu(u)
    v = _layer_norm(jax.nn.gelu(v), ln_g, ln_b)
    vc = v.reshape(bsz, s // CHUNK, CHUNK, GMLP_HEADS, GMLP_HEAD_DIM)
    mask = jnp.tril(jnp.ones((CHUNK, CHUNK), dtype=bool))
    wsm = jnp.where(mask[None], ws, jnp.zeros_like(ws))
    mixed = jnp.einsum('hts,bcshd->bcthd', wsm, vc) + bs.T[None, None, :, :, None]
    return u * mixed.reshape(bsz, s, D_MODEL)


PEER_G = 8
PEER_ROWS = PEER_HEADS * TOPK


def _peer_gather_kernel(idx_ref, x_ref, gate_ref, g2_ref, b2_ref, uv_hbm, o_ref, buf, p_ref, sem):
    i = pl.program_id(0)
    nb = pl.num_programs(0) - 1
    slot = lax.rem(i, 2)

    @pl.when(i < nb)
    def _():
        def issue(g, c):
            for k in range(PEER_ROWS):
                e = idx_ref[0, 0, g * PEER_ROWS + k]
                pltpu.make_async_copy(uv_hbm.at[e], buf.at[slot, g, :, :, k, :],
                                      sem.at[slot]).start(priority=k % 2)
            return c
        lax.fori_loop(0, PEER_G, issue, 0)

    @pl.when(i > 0)
    def _():
        prev = 1 - slot
        pltpu.make_async_copy(buf.at[prev], buf.at[prev], sem.at[prev]).wait()

        gate_t = gate_ref[...].T
        lane = lax.broadcasted_iota(jnp.int32, gate_t.shape, 1)
        for g in range(PEER_G):
            acc = buf[prev, g, 0, 0] * x_ref[g:g + 1, 0:LANES]
            for c in range(1, D_TILES):
                acc = acc + buf[prev, g, 0, c] * x_ref[g:g + 1, c * LANES:(c + 1) * LANES]
            h = jnp.sum(acc, axis=1, keepdims=True)
            gcol = jnp.sum(jnp.where(lane == g, gate_t, 0.0), axis=1, keepdims=True)
            act = jax.nn.gelu(h) * gcol
            for c in range(D_TILES):
                p_ref[g:g + 1, c * LANES:(c + 1) * LANES] = jnp.sum(
                    buf[prev, g, 1, c] * act, axis=0, keepdims=True)

        y = ALPHA * x_ref[...] + p_ref[...]
        mu = jnp.mean(y, axis=-1, keepdims=True)
        yc = y - mu
        var = jnp.mean(yc * yc, axis=-1, keepdims=True)
        o_ref[...] = yc * lax.rsqrt(var + LN_EPS) * g2_ref[...] + b2_ref[...]


def _peer_gather(x1, idx, gate, u_tab, v_tab, ln_g, ln_b):
    t, d = x1.shape
    nb = t // PEER_G
    n_exp = u_tab.shape[0]
    uv = jnp.stack([u_tab, v_tab], axis=1).reshape(n_exp, 2, D_TILES, LANES)
    idx3 = idx.reshape(nb, 1, PEER_G * PEER_ROWS)
    cur = lambda i: (jnp.minimum(i, nb - 1), 0, 0)
    prv = lambda i: (jnp.maximum(i - 1, 0), 0)
    return pl.pallas_call(
        _peer_gather_kernel,
        grid=(nb + 1,),
        in_specs=[
            pl.BlockSpec((1, 1, PEER_G * PEER_ROWS), cur, memory_space=pltpu.SMEM),
            pl.BlockSpec((PEER_G, d), prv),
            pl.BlockSpec((PEER_G, PEER_ROWS), prv),
            pl.BlockSpec((1, d), lambda i: (0, 0)),
            pl.BlockSpec((1, d), lambda i: (0, 0)),
            pl.BlockSpec(memory_space=pl.ANY),
        ],
        out_specs=pl.BlockSpec((PEER_G, d), prv),
        out_shape=jax.ShapeDtypeStruct((t, d), jnp.float32),
        scratch_shapes=[
            pltpu.VMEM((2, PEER_G, 2, D_TILES, PEER_ROWS, LANES), jnp.float32),
            pltpu.VMEM((PEER_G, d), jnp.float32),
            pltpu.SemaphoreType.DMA((2,)),
        ],
        compiler_params=pltpu.CompilerParams(dimension_semantics=("arbitrary",),
                                             vmem_limit_bytes=40 * 1024 * 1024),
        name="peer_gather",
    )(idx3, x1, gate, ln_g.reshape(1, d), ln_b.reshape(1, d), uv)


def _peer(xt, wq, sub_keys, u_tab, v_tab, ln_g, ln_b):
    t, d = xt.shape
    q = _matmul(xt, wq.astype(jnp.bfloat16)).reshape(t, PEER_HEADS, 2, D_HALF)
    scores = jnp.einsum('thpd,hpnd->thpn', q, sub_keys)
    s1, i1 = lax.top_k(scores[:, :, 0], TOPK)
    s2, i2 = lax.top_k(scores[:, :, 1], TOPK)
    cand_s = (s1[..., :, None] + s2[..., None, :]).reshape(t, PEER_HEADS, TOPK * TOPK)
    cand_i = (i1[..., :, None] * N_KEYS + i2[..., None, :]).reshape(t, PEER_HEADS, TOPK * TOPK)
    top_s, pos = lax.top_k(cand_s, TOPK)
    idx = jnp.take_along_axis(cand_i, pos, axis=-1)
    gate = jax.nn.softmax(top_s, axis=-1)
    return _peer_gather(xt, idx.reshape(t, PEER_ROWS), gate.reshape(t, PEER_ROWS),
                        u_tab, v_tab, ln_g, ln_b)


def kernel(x, w_in, b_gate, conv_w, conv_b, rg_w_a, rg_b_a, rg_w_x, rg_b_x, rg_lambda, gm_ln_g, gm_ln_b, gm_ws, gm_bs, w_out, ln1_g, ln1_b, peer_wq, peer_sub_keys, peer_u, peer_v, ln2_g, ln2_b):
    bsz, s, d = x.shape
    t = bsz * s
    l = 0
    xt = x.reshape(t, d)
    proj = _matmul(xt, w_in[l].astype(jnp.bfloat16)).reshape(bsz, s, -1)
    xa, ga, u, v, g = jnp.split(proj, [d, 2 * d, 3 * d, 4 * d], axis=-1)
    ya = jax.nn.gelu(ga) * _rg_lru(_conv(xa, conv_w[l], conv_b[l]),
                                   rg_w_a[l], rg_b_a[l], rg_w_x[l], rg_b_x[l], rg_lambda[l])
    yb = _gating(u, v, gm_ln_g[l], gm_ln_b[l], gm_ws[l], gm_bs[l])
    gates = jax.nn.sigmoid(g.reshape(bsz, s, N_BRANCH, d) + b_gate[l])
    y = gates[:, :, 0] * ya + gates[:, :, 1] * yb
    o = _matmul(y.reshape(t, d), w_out[l].astype(jnp.bfloat16))
    x1 = _layer_norm(ALPHA * xt + o, ln1_g[l], ln1_b[l])
    x2 = _peer(x1, peer_wq[l], peer_sub_keys[l], peer_u[l], peer_v[l], ln2_g[l], ln2_b[l])
    return x2.reshape(bsz, s, d)
```

```python
import jax
import jax.numpy as jnp
from jax import lax
from jax.experimental import pallas as pl
from jax.experimental.pallas import tpu as pltpu

D_MODEL = 1024
RNN_HEADS = 8
CONV_WIDTH = 4
LRU_C = 8.0
GMLP_HEADS = 8
CHUNK = 128
N_BRANCH = 2
PEER_HEADS = 8
N_KEYS = 128
D_HALF = 128
TOPK = 16
ALPHA = 2.0 ** 0.25
LN_EPS = 1e-5
NEG_INF = float("-inf")

LANES = 128
SUBLANES = 8
D_TILES = D_MODEL // LANES
PEER_ROWS = PEER_HEADS * TOPK

MIX_TM = 256
ROUTE_TM = 256
PEER_G = 16
MIX_VMEM_BYTES = 56 * 1024 * 1024
PEER_VMEM_BYTES = 40 * 1024 * 1024


def _ln_rows(y, g, b):
    mu = jnp.mean(y, axis=-1, keepdims=True)
    yc = y - mu
    var = jnp.mean(yc * yc, axis=-1, keepdims=True)
    return yc * lax.rsqrt(var + LN_EPS) * g + b


def _mix_kernel(x_ref, win_ref, bgate_ref, convw_ref, convb_ref, wax_ref, ba_ref, bx_ref, lam_ref,
                lng_ref, lnb_ref, ws_ref, bsf_ref, wout_ref, g1_ref, b1_ref,
                o_ref, xa_buf, h_ref):
    tm = x_ref.shape[0]
    d = D_MODEL
    hd = d // RNN_HEADS
    j = pl.program_id(1)

    @pl.when(j == 0)
    def _():
        xa_buf[0:SUBLANES, :] = jnp.zeros((SUBLANES, d), jnp.float32)
        h_ref[...] = jnp.zeros_like(h_ref)

    x = x_ref[...]
    xb = x.astype(jnp.bfloat16)

    def proj(k):
        return jnp.dot(xb, win_ref[:, k * d:(k + 1) * d], preferred_element_type=jnp.float32)

    xa_buf[SUBLANES:SUBLANES + tm, :] = proj(0)
    xc = convb_ref[...] + convw_ref[CONV_WIDTH - 1:CONV_WIDTH, :] * xa_buf[SUBLANES:SUBLANES + tm, :]
    for k in range(1, CONV_WIDTH):
        xc = xc + convw_ref[CONV_WIDTH - 1 - k:CONV_WIDTH - k, :] * xa_buf[SUBLANES - k:SUBLANES - k + tm, :]
    xa_buf[0:SUBLANES, :] = xa_buf[tm:tm + SUBLANES, :]

    xcb = xc.astype(jnp.bfloat16)
    pre = [jnp.dot(xcb[:, hh * hd:(hh + 1) * hd], wax_ref[hh], preferred_element_type=jnp.float32)
           for hh in range(RNN_HEADS)]
    r = jax.nn.sigmoid(jnp.concatenate([p[:, :hd] for p in pre], axis=1) + ba_ref[...])
    ig = jax.nn.sigmoid(jnp.concatenate([p[:, hd:] for p in pre], axis=1) + bx_ref[...])
    z = -lam_ref[...]
    softplus = jnp.maximum(z, 0.0) + jnp.log1p(jnp.exp(-jnp.abs(z)))
    log_a = (-LRU_C) * r * softplus
    a = jnp.exp(log_a)
    b = jnp.sqrt(1.0 - a * a) * (ig * xc)

    row = lax.broadcasted_iota(jnp.int32, (tm, d), 0)
    sh = 1
    while sh < tm:
        a_prev = jnp.where(row >= sh, pltpu.roll(a, sh, axis=0), 1.0)
        b_prev = jnp.where(row >= sh, pltpu.roll(b, sh, axis=0), 0.0)
        b = a * b_prev + b
        a = a * a_prev
        sh *= 2
    h = b + a * h_ref[...]
    h_ref[...] = h[tm - 1:tm, :]
    y = jax.nn.sigmoid(proj(4) + bgate_ref[0:1, :]) * (jax.nn.gelu(proj(1)) * h)

    vn = _ln_rows(jax.nn.gelu(proj(3)), lng_ref[...], lnb_ref[...]).astype(jnp.bfloat16)
    tri = (lax.broadcasted_iota(jnp.int32, (CHUNK, CHUNK), 0)
           >= lax.broadcasted_iota(jnp.int32, (CHUNK, CHUNK), 1))
    chunks = []
    for c in range(tm // CHUNK):
        heads = []
        for hh in range(GMLP_HEADS):
            w = jnp.where(tri, ws_ref[hh], 0.0).astype(jnp.bfloat16)
            heads.append(jnp.dot(w, vn[c * CHUNK:(c + 1) * CHUNK, hh * hd:(hh + 1) * hd],
                                 preferred_element_type=jnp.float32))
        chunks.append(jnp.concatenate(heads, axis=1) + bsf_ref[...])
    mixed = jnp.concatenate(chunks, axis=0)
    y = y + jax.nn.sigmoid(proj(5) + bgate_ref[1:2, :]) * (jax.nn.gelu(proj(2)) * mixed)

    o = jnp.dot(y.astype(jnp.bfloat16), wout_ref[...], preferred_element_type=jnp.float32)
    o_ref[...] = _ln_rows(ALPHA * x + o, g1_ref[...], b1_ref[...])


def _token_mix(x, w_in, b_gate, conv_w, conv_b, rg_w_a, rg_b_a, rg_w_x, rg_b_x, rg_lambda,
               gm_ln_g, gm_ln_b, gm_ws, gm_bs, w_out, ln1_g, ln1_b, tm=MIX_TM):
    bsz, s, d = x.shape
    nj = s // tm
    row = lambda v: v.reshape(1, d)
    wax = jnp.concatenate([rg_w_a, rg_w_x], axis=-1).astype(jnp.bfloat16)
    bs_full = jnp.repeat(gm_bs.T, d // GMLP_HEADS, axis=1)
    const2 = lambda shape: pl.BlockSpec(shape, lambda b, j: (0, 0))
    const3 = lambda shape: pl.BlockSpec(shape, lambda b, j: (0, 0, 0))
    return pl.pallas_call(
        _mix_kernel,
        grid=(bsz, nj),
        in_specs=[
            pl.BlockSpec((tm, d), lambda b, j: (b * nj + j, 0)),
            const2((d, w_in.shape[1])),
            const2((N_BRANCH, d)),
            const2((CONV_WIDTH, d)),
            const2((1, d)),
            const3(wax.shape),
            const2((1, d)), const2((1, d)), const2((1, d)),
            const2((1, d)), const2((1, d)),
            const3(gm_ws.shape),
            const2((CHUNK, d)),
            const2((d, d)),
            const2((1, d)), const2((1, d)),
        ],
        out_specs=pl.BlockSpec((tm, d), lambda b, j: (b * nj + j, 0)),
        out_shape=jax.ShapeDtypeStruct((bsz * s, d), jnp.float32),
        scratch_shapes=[pltpu.VMEM((tm + SUBLANES, d), jnp.float32),
                        pltpu.VMEM((1, d), jnp.float32)],
        compiler_params=pltpu.CompilerParams(dimension_semantics=("arbitrary", "arbitrary"),
                                             vmem_limit_bytes=MIX_VMEM_BYTES),
        name="token_mix",
    )(x.reshape(bsz * s, d), w_in.astype(jnp.bfloat16), b_gate, conv_w, row(conv_b), wax,
      row(rg_b_a), row(rg_b_x), row(rg_lambda), row(gm_ln_g), row(gm_ln_b), gm_ws, bs_full,
      w_out.astype(jnp.bfloat16), row(ln1_g), row(ln1_b))


def _extract_top(s, ids, rounds):
    rows = s.shape[0]
    pos = lax.broadcasted_iota(jnp.int32, s.shape, 0)
    vals, picked = [], []
    for _ in range(rounds):
        m = jnp.max(s, axis=0, keepdims=True)
        p = jnp.min(jnp.where(s == m, pos, rows), axis=0, keepdims=True)
        hit = pos == p
        vals.append(m)
        picked.append(p if ids is None else jnp.max(jnp.where(hit, ids, -1), axis=0, keepdims=True))
        s = jnp.where(hit, NEG_INF, s)
    return vals, picked


def _route_kernel(x_ref, wq_ref, keys_ref, idx_ref, gate_ref, s_ref, it_ref, gt_ref):
    tm = x_ref.shape[0]
    q = jnp.dot(x_ref[...].astype(jnp.bfloat16), wq_ref[...], preferred_element_type=jnp.float32)
    for j in range(2 * PEER_HEADS):
        s_ref[j] = lax.dot_general(keys_ref[j], q[:, j * D_HALF:(j + 1) * D_HALF].astype(jnp.bfloat16),
                                   (((1,), (1,)), ((), ())), preferred_element_type=jnp.float32)

    sub = lax.broadcasted_iota(jnp.int32, (SUBLANES, LANES), 0)

    def tile(tl, carry):
        c0 = pl.multiple_of(tl * LANES, LANES)

        def head(h, c):
            v1, i1 = _extract_top(s_ref[2 * h, :, pl.ds(c0, LANES)], None, TOPK)
            v2, i2 = _extract_top(s_ref[2 * h + 1, :, pl.ds(c0, LANES)], None, TOPK)
            v2lo = jnp.concatenate(v2[:SUBLANES], axis=0)
            v2hi = jnp.concatenate(v2[SUBLANES:], axis=0)
            i2lo = jnp.concatenate(i2[:SUBLANES], axis=0)
            i2hi = jnp.concatenate(i2[SUBLANES:], axis=0)
            cs = [v1[0] + v2lo, v1[0] + v2hi]
            ci = [i1[0] * N_KEYS + i2lo, i1[0] * N_KEYS + i2hi]
            for a in range(1, TOPK):
                keep = sub < (TOPK // (a + 1))
                cs.append(jnp.where(keep, v1[a] + v2lo, NEG_INF))
                ci.append(i1[a] * N_KEYS + i2lo)
            tv, te = _extract_top(jnp.concatenate(cs, axis=0), jnp.concatenate(ci, axis=0), TOPK)
            tv = jnp.concatenate(tv, axis=0)
            ex = jnp.exp(tv - tv[0:1])
            g = ex / jnp.sum(ex, axis=0, keepdims=True)
            r0 = pl.multiple_of(h * TOPK, TOPK)
            gt_ref[pl.ds(r0, TOPK), :] = g
            it_ref[pl.ds(r0, TOPK), :] = jnp.concatenate(te, axis=0)
            return c

        lax.fori_loop(0, PEER_HEADS, head, 0)
        gate_ref[pl.ds(c0, LANES), :] = gt_ref[...].T
        idx_ref[pl.ds(c0, LANES), :] = it_ref[...].T
        return carry

    lax.fori_loop(0, tm // LANES, tile, 0)


def _peer_route(x1, wq_bf16, keys_bf16, tm=ROUTE_TM):
    t, d = x1.shape
    nq = wq_bf16.shape[1]
    return pl.pallas_call(
        _route_kernel,
        grid=(t // tm,),
        in_specs=[pl.BlockSpec((tm, d), lambda i: (i, 0)),
                  pl.BlockSpec((d, nq), lambda i: (0, 0)),
                  pl.BlockSpec((2 * PEER_HEADS, N_KEYS, D_HALF), lambda i: (0, 0, 0))],
        out_specs=[pl.BlockSpec((tm, PEER_ROWS), lambda i: (i, 0)),
                   pl.BlockSpec((tm, PEER_ROWS), lambda i: (i, 0))],
        out_shape=[jax.ShapeDtypeStruct((t, PEER_ROWS), jnp.int32),
                   jax.ShapeDtypeStruct((t, PEER_ROWS), jnp.float32)],
        scratch_shapes=[pltpu.VMEM((2 * PEER_HEADS, N_KEYS, tm), jnp.float32),
                        pltpu.VMEM((PEER_ROWS, LANES), jnp.int32),
                        pltpu.VMEM((PEER_ROWS, LANES), jnp.float32)],
        compiler_params=pltpu.CompilerParams(dimension_semantics=("arbitrary",)),
        name="peer_route",
    )(x1, wq_bf16, keys_bf16)


def _peer_gather_kernel(idx_ref, x_ref, gate_ref, g2_ref, b2_ref, uv_hbm, o_ref, buf, p_ref, sem):
    i = pl.program_id(0)
    nb = pl.num_programs(0) - 1
    slot = lax.rem(i, 2)

    @pl.when(i < nb)
    def _():
        def issue(g, c):
            for k in range(PEER_ROWS):
                e = idx_ref[0, 0, g * PEER_ROWS + k]
                pltpu.make_async_copy(uv_hbm.at[e], buf.at[slot, g, :, :, k, :],
                                      sem.at[slot]).start(priority=k % 2)
            return c
        lax.fori_loop(0, PEER_G, issue, 0)

    @pl.when(i > 0)
    def _():
        prev = 1 - slot
        pltpu.make_async_copy(buf.at[prev], buf.at[prev], sem.at[prev]).wait()

        gate_t = gate_ref[...].T
        lane = lax.broadcasted_iota(jnp.int32, gate_t.shape, 1)
        for g in range(PEER_G):
            acc = buf[prev, g, 0, 0] * x_ref[g:g + 1, 0:LANES]
            for c in range(1, D_TILES):
                acc = acc + buf[prev, g, 0, c] * x_ref[g:g + 1, c * LANES:(c + 1) * LANES]
            h = jnp.sum(acc, axis=1, keepdims=True)
            gcol = jnp.sum(jnp.where(lane == g, gate_t, 0.0), axis=1, keepdims=True)
            act = jax.nn.gelu(h) * gcol
            for c in range(D_TILES):
                p_ref[g:g + 1, c * LANES:(c + 1) * LANES] = jnp.sum(
                    buf[prev, g, 1, c] * act, axis=0, keepdims=True)

        y = ALPHA * x_ref[...] + p_ref[...]
        mu = jnp.mean(y, axis=-1, keepdims=True)
        yc = y - mu
        var = jnp.mean(yc * yc, axis=-1, keepdims=True)
        o_ref[...] = yc * lax.rsqrt(var + LN_EPS) * g2_ref[...] + b2_ref[...]


def _peer_gather(x1, idx, gate, u_tab, v_tab, ln_g, ln_b):
    t, d = x1.shape
    nb = t // PEER_G
    n_exp = u_tab.shape[0]
    uv = jnp.stack([u_tab, v_tab], axis=1).reshape(n_exp, 2, D_TILES, LANES)
    idx3 = idx.reshape(nb, 1, PEER_G * PEER_ROWS)
    cur = lambda i: (jnp.minimum(i, nb - 1), 0, 0)
    prv = lambda i: (jnp.maximum(i - 1, 0), 0)
    return pl.pallas_call(
        _peer_gather_kernel,
        grid=(nb + 1,),
        in_specs=[
            pl.BlockSpec((1, 1, PEER_G * PEER_ROWS), cur, memory_space=pltpu.SMEM),
            pl.BlockSpec((PEER_G, d), prv),
            pl.BlockSpec((PEER_G, PEER_ROWS), prv),
            pl.BlockSpec((1, d), lambda i: (0, 0)),
            pl.BlockSpec((1, d), lambda i: (0, 0)),
            pl.BlockSpec(memory_space=pl.ANY),
        ],
        out_specs=pl.BlockSpec((PEER_G, d), prv),
        out_shape=jax.ShapeDtypeStruct((t, d), jnp.float32),
        scratch_shapes=[
            pltpu.VMEM((2, PEER_G, 2, D_TILES, PEER_ROWS, LANES), jnp.float32),
            pltpu.VMEM((PEER_G, d), jnp.float32),
            pltpu.SemaphoreType.DMA((2,)),
        ],
        compiler_params=pltpu.CompilerParams(dimension_semantics=("arbitrary",),
                                             vmem_limit_bytes=PEER_VMEM_BYTES),
        name="peer_gather",
    )(idx3, x1, gate, ln_g.reshape(1, d), ln_b.reshape(1, d), uv)


def kernel(x, w_in, b_gate, conv_w, conv_b, rg_w_a, rg_b_a, rg_w_x, rg_b_x, rg_lambda, gm_ln_g, gm_ln_b, gm_ws, gm_bs, w_out, ln1_g, ln1_b, peer_wq, peer_sub_keys, peer_u, peer_v, ln2_g, ln2_b):
    bsz, s, d = x.shape
    assert w_in.shape[0] == 1, "single-layer stack"
    l = 0
    x1 = _token_mix(x, w_in[l], b_gate[l], conv_w[l], conv_b[l], rg_w_a[l], rg_b_a[l], rg_w_x[l], rg_b_x[l],
                    rg_lambda[l], gm_ln_g[l], gm_ln_b[l], gm_ws[l], gm_bs[l], w_out[l], ln1_g[l], ln1_b[l])
    keys = peer_sub_keys[l].reshape(2 * PEER_HEADS, N_KEYS, D_HALF).astype(jnp.bfloat16)
    idx, gate = _peer_route(x1, peer_wq[l].astype(jnp.bfloat16), keys)
    x2 = _peer_gather(x1, idx, gate, peer_u[l], peer_v[l], ln2_g[l], ln2_b[l])
    return x2.reshape(bsz, s, d)
```

```python
import jax
import jax.numpy as jnp
from jax import lax
from jax.experimental import pallas as pl
from jax.experimental.pallas import tpu as pltpu

D_MODEL = 1024
RNN_HEADS = 8
CONV_WIDTH = 4
LRU_C = 8.0
GMLP_HEADS = 8
CHUNK = 128
N_BRANCH = 2
PEER_HEADS = 8
N_KEYS = 128
D_HALF = 128
TOPK = 16
ALPHA = 2.0 ** 0.25
LN_EPS = 1e-5
NEG_INF = float("-inf")

LANES = 128
SUBLANES = 8
D_TILES = D_MODEL // LANES
PEER_ROWS = PEER_HEADS * TOPK

MIX_TM = 256
ROUTE_TM = 256
ROUTE_W = 256
PEER_G = 16
MIX_VMEM_BYTES = 56 * 1024 * 1024
PEER_VMEM_BYTES = 48 * 1024 * 1024


def _ln_rows(y, g, b):
    mu = jnp.mean(y, axis=-1, keepdims=True)
    yc = y - mu
    var = jnp.mean(yc * yc, axis=-1, keepdims=True)
    return yc * lax.rsqrt(var + LN_EPS) * g + b


def _mix_kernel(x_ref, win_ref, bgate_ref, convw_ref, convb_ref, wax_ref, ba_ref, bx_ref, lam_ref,
                lng_ref, lnb_ref, ws_ref, bsf_ref, wout_ref, g1_ref, b1_ref,
                o_ref, xa_buf, h_ref):
    tm = x_ref.shape[0]
    d = D_MODEL
    hd = d // RNN_HEADS
    j = pl.program_id(1)

    @pl.when(j == 0)
    def _():
        xa_buf[0:SUBLANES, :] = jnp.zeros((SUBLANES, d), jnp.float32)
        h_ref[...] = jnp.zeros_like(h_ref)

    x = x_ref[...]
    xb = x.astype(jnp.bfloat16)

    def proj(k):
        return jnp.dot(xb, win_ref[:, k * d:(k + 1) * d], preferred_element_type=jnp.float32)

    xa_buf[SUBLANES:SUBLANES + tm, :] = proj(0)
    xc = convb_ref[...] + convw_ref[CONV_WIDTH - 1:CONV_WIDTH, :] * xa_buf[SUBLANES:SUBLANES + tm, :]
    for k in range(1, CONV_WIDTH):
        xc = xc + convw_ref[CONV_WIDTH - 1 - k:CONV_WIDTH - k, :] * xa_buf[SUBLANES - k:SUBLANES - k + tm, :]
    xa_buf[0:SUBLANES, :] = xa_buf[tm:tm + SUBLANES, :]

    xcb = xc.astype(jnp.bfloat16)
    pre = [jnp.dot(xcb[:, hh * hd:(hh + 1) * hd], wax_ref[hh], preferred_element_type=jnp.float32)
           for hh in range(RNN_HEADS)]
    r = jax.nn.sigmoid(jnp.concatenate([p[:, :hd] for p in pre], axis=1) + ba_ref[...])
    ig = jax.nn.sigmoid(jnp.concatenate([p[:, hd:] for p in pre], axis=1) + bx_ref[...])
    z = -lam_ref[...]
    softplus = jnp.maximum(z, 0.0) + jnp.log1p(jnp.exp(-jnp.abs(z)))
    log_a = (-LRU_C) * r * softplus
    a = jnp.exp(log_a)
    b = jnp.sqrt(1.0 - a * a) * (ig * xc)

    row = lax.broadcasted_iota(jnp.int32, (tm, d), 0)
    sh = 1
    while sh < tm:
        a_prev = jnp.where(row >= sh, pltpu.roll(a, sh, axis=0), 1.0)
        b_prev = jnp.where(row >= sh, pltpu.roll(b, sh, axis=0), 0.0)
        b = a * b_prev + b
        a = a * a_prev
        sh *= 2
    h = b + a * h_ref[...]
    h_ref[...] = h[tm - 1:tm, :]
    y = jax.nn.sigmoid(proj(4) + bgate_ref[0:1, :]) * (jax.nn.gelu(proj(1)) * h)

    vn = _ln_rows(jax.nn.gelu(proj(3)), lng_ref[...], lnb_ref[...]).astype(jnp.bfloat16)
    tri = (lax.broadcasted_iota(jnp.int32, (CHUNK, CHUNK), 0)
           >= lax.broadcasted_iota(jnp.int32, (CHUNK, CHUNK), 1))
    chunks = []
    for c in range(tm // CHUNK):
        heads = []
        for hh in range(GMLP_HEADS):
            w = jnp.where(tri, ws_ref[hh], 0.0).astype(jnp.bfloat16)
            heads.append(jnp.dot(w, vn[c * CHUNK:(c + 1) * CHUNK, hh * hd:(hh + 1) * hd],
                                 preferred_element_type=jnp.float32))
        chunks.append(jnp.concatenate(heads, axis=1) + bsf_ref[...])
    mixed = jnp.concatenate(chunks, axis=0)
    y = y + jax.nn.sigmoid(proj(5) + bgate_ref[1:2, :]) * (jax.nn.gelu(proj(2)) * mixed)

    o = jnp.dot(y.astype(jnp.bfloat16), wout_ref[...], preferred_element_type=jnp.float32)
    o_ref[...] = _ln_rows(ALPHA * x + o, g1_ref[...], b1_ref[...])


def _token_mix(x, w_in, b_gate, conv_w, conv_b, rg_w_a, rg_b_a, rg_w_x, rg_b_x, rg_lambda,
               gm_ln_g, gm_ln_b, gm_ws, gm_bs, w_out, ln1_g, ln1_b, tm=MIX_TM):
    bsz, s, d = x.shape
    nj = s // tm
    row = lambda v: v.reshape(1, d)
    wax = jnp.concatenate([rg_w_a, rg_w_x], axis=-1).astype(jnp.bfloat16)
    bs_full = jnp.repeat(gm_bs.T, d // GMLP_HEADS, axis=1)
    const2 = lambda shape: pl.BlockSpec(shape, lambda b, j: (0, 0))
    const3 = lambda shape: pl.BlockSpec(shape, lambda b, j: (0, 0, 0))
    return pl.pallas_call(
        _mix_kernel,
        grid=(bsz, nj),
        in_specs=[
            pl.BlockSpec((tm, d), lambda b, j: (b * nj + j, 0)),
            const2((d, w_in.shape[1])),
            const2((N_BRANCH, d)),
            const2((CONV_WIDTH, d)),
            const2((1, d)),
            const3(wax.shape),
            const2((1, d)), const2((1, d)), const2((1, d)),
            const2((1, d)), const2((1, d)),
            const3(gm_ws.shape),
            const2((CHUNK, d)),
            const2((d, d)),
            const2((1, d)), const2((1, d)),
        ],
        out_specs=pl.BlockSpec((tm, d), lambda b, j: (b * nj + j, 0)),
        out_shape=jax.ShapeDtypeStruct((bsz * s, d), jnp.float32),
        scratch_shapes=[pltpu.VMEM((tm + SUBLANES, d), jnp.float32),
                        pltpu.VMEM((1, d), jnp.float32)],
        compiler_params=pltpu.CompilerParams(dimension_semantics=("arbitrary", "arbitrary"),
                                             vmem_limit_bytes=MIX_VMEM_BYTES),
        name="token_mix",
    )(x.reshape(bsz * s, d), w_in.astype(jnp.bfloat16), b_gate, conv_w, row(conv_b), wax,
      row(rg_b_a), row(rg_b_x), row(rg_lambda), row(gm_ln_g), row(gm_ln_b), gm_ws, bs_full,
      w_out.astype(jnp.bfloat16), row(ln1_g), row(ln1_b))


def _extract_top(s, ids, rounds):
    rows = s.shape[0]
    pos = lax.broadcasted_iota(jnp.int32, s.shape, 0)
    vals, picked = [], []
    for _ in range(rounds):
        m = jnp.max(s, axis=0, keepdims=True)
        p = jnp.min(jnp.where(s == m, pos, rows), axis=0, keepdims=True)
        hit = pos == p
        vals.append(m)
        picked.append(p if ids is None else jnp.max(jnp.where(hit, ids, -1), axis=0, keepdims=True))
        s = jnp.where(hit, NEG_INF, s)
    return vals, picked


def _route_kernel(x_ref, wq_ref, keys_ref, idx_ref, gate_ref, s_ref, it_ref, gt_ref):
    tm = x_ref.shape[0]
    q = jnp.dot(x_ref[...].astype(jnp.bfloat16), wq_ref[...], preferred_element_type=jnp.float32)
    for j in range(2 * PEER_HEADS):
        s_ref[j] = lax.dot_general(keys_ref[j], q[:, j * D_HALF:(j + 1) * D_HALF].astype(jnp.bfloat16),
                                   (((1,), (1,)), ((), ())), preferred_element_type=jnp.float32)

    sub = lax.broadcasted_iota(jnp.int32, (SUBLANES, ROUTE_W), 0)

    def tile(tl, carry):
        c0 = pl.multiple_of(tl * ROUTE_W, ROUTE_W)

        def head(h, c):
            v1, i1 = _extract_top(s_ref[2 * h, :, pl.ds(c0, ROUTE_W)], None, TOPK)
            v2, i2 = _extract_top(s_ref[2 * h + 1, :, pl.ds(c0, ROUTE_W)], None, TOPK)
            v2lo = jnp.concatenate(v2[:SUBLANES], axis=0)
            v2hi = jnp.concatenate(v2[SUBLANES:], axis=0)
            i2lo = jnp.concatenate(i2[:SUBLANES], axis=0)
            i2hi = jnp.concatenate(i2[SUBLANES:], axis=0)
            cs = [v1[0] + v2lo, v1[0] + v2hi]
            ci = [i1[0] * N_KEYS + i2lo, i1[0] * N_KEYS + i2hi]
            for a in range(1, TOPK):
                keep = sub < (TOPK // (a + 1))
                cs.append(jnp.where(keep, v1[a] + v2lo, NEG_INF))
                ci.append(i1[a] * N_KEYS + i2lo)
            tv, te = _extract_top(jnp.concatenate(cs, axis=0), jnp.concatenate(ci, axis=0), TOPK)
            tv = jnp.concatenate(tv, axis=0)
            ex = jnp.exp(tv - tv[0:1])
            g = ex / jnp.sum(ex, axis=0, keepdims=True)
            r0 = pl.multiple_of(h * TOPK, TOPK)
            gt_ref[pl.ds(r0, TOPK), :] = g
            it_ref[pl.ds(r0, TOPK), :] = jnp.concatenate(te, axis=0)
            return c

        lax.fori_loop(0, PEER_HEADS, head, 0)
        gate_ref[pl.ds(c0, ROUTE_W), :] = gt_ref[...].T
        idx_ref[pl.ds(c0, ROUTE_W), :] = it_ref[...].T
        return carry

    lax.fori_loop(0, tm // ROUTE_W, tile, 0)


def _peer_route(x1, wq_bf16, keys_bf16, tm=ROUTE_TM):
    t, d = x1.shape
    nq = wq_bf16.shape[1]
    return pl.pallas_call(
        _route_kernel,
        grid=(t // tm,),
        in_specs=[pl.BlockSpec((tm, d), lambda i: (i, 0)),
                  pl.BlockSpec((d, nq), lambda i: (0, 0)),
                  pl.BlockSpec((2 * PEER_HEADS, N_KEYS, D_HALF), lambda i: (0, 0, 0))],
        out_specs=[pl.BlockSpec((tm, PEER_ROWS), lambda i: (i, 0)),
                   pl.BlockSpec((tm, PEER_ROWS), lambda i: (i, 0))],
        out_shape=[jax.ShapeDtypeStruct((t, PEER_ROWS), jnp.int32),
                   jax.ShapeDtypeStruct((t, PEER_ROWS), jnp.float32)],
        scratch_shapes=[pltpu.VMEM((2 * PEER_HEADS, N_KEYS, tm), jnp.float32),
                        pltpu.VMEM((PEER_ROWS, ROUTE_W), jnp.int32),
                        pltpu.VMEM((PEER_ROWS, ROUTE_W), jnp.float32)],
        compiler_params=pltpu.CompilerParams(dimension_semantics=("arbitrary",)),
        name="peer_route",
    )(x1, wq_bf16, keys_bf16)


def _sublane_sums(tiles, sub):
    step = SUBLANES // 2
    while step >= 1:
        low = (sub & step) == 0
        nxt = []
        for j in range(len(tiles) // 2):
            a, b = tiles[j], tiles[j + len(tiles) // 2]
            nxt.append(jnp.where(low, a, pltpu.roll(b, step, axis=0))
                       + jnp.where(low, pltpu.roll(a, SUBLANES - step, axis=0), b))
        tiles = nxt
        step //= 2
    return tiles[0]


def _peer_gather_kernel(idx_ref, x_ref, gate_ref, g2_ref, b2_ref, uv_hbm, o_ref, buf, sem):
    i = pl.program_id(0)
    nb = pl.num_programs(0) - 1
    slot = lax.rem(i, 2)
    prev = 1 - slot

    def start_rows(g, ks):
        for k in ks:
            e = idx_ref[0, 0, g * PEER_ROWS + k]
            pltpu.make_async_copy(uv_hbm.at[e], buf.at[slot, g, k], sem.at[slot]).start(priority=k % 2)

    def token(g, gate_t, lane, sub, fetch):
        x8 = jnp.concatenate([x_ref[g:g + 1, c * LANES:(c + 1) * LANES] for c in range(D_TILES)], axis=0)
        parts = []
        for q in range(PEER_ROWS // SUBLANES):
            prods = [buf[prev, g, q * SUBLANES + j, 0] * x8 for j in range(SUBLANES)]
            fetch(g, q)
            parts.append(_sublane_sums(prods, sub))
        acc = jnp.concatenate(parts, axis=0)
        h = jnp.sum(acc, axis=1, keepdims=True)
        gcol = jnp.sum(jnp.where(lane == g, gate_t, 0.0), axis=1, keepdims=True)
        act = jax.nn.gelu(h) * gcol
        out = None
        for k in range(PEER_ROWS):
            term = buf[prev, g, k, 1] * act[k:k + 1, :]
            out = term if out is None else out + term
        p = ALPHA * x8 + out
        mu = jnp.mean(p, keepdims=True)
        pc = p - mu
        var = jnp.mean(pc * pc, keepdims=True)
        y8 = pc * lax.rsqrt(var + LN_EPS) * g2_ref[...] + b2_ref[...]
        for c in range(D_TILES):
            o_ref[g:g + 1, c * LANES:(c + 1) * LANES] = y8[c:c + 1, :]

    def finish_block(fetch):
        pltpu.make_async_copy(buf.at[prev], buf.at[prev], sem.at[prev]).wait()
        gate_t = gate_ref[...].T
        lane = lax.broadcasted_iota(jnp.int32, gate_t.shape, 1)
        sub = lax.broadcasted_iota(jnp.int32, (SUBLANES, LANES), 0)
        for g in range(PEER_G):
            token(g, gate_t, lane, sub, fetch)

    part = SUBLANES

    @pl.when(i == 0)
    def _():
        def first(g, c):
            start_rows(g, range(PEER_ROWS))
            return c
        lax.fori_loop(0, PEER_G, first, 0)

    @pl.when(jnp.logical_and(i > 0, i < nb))
    def _():
        finish_block(lambda g, j: start_rows(g, range(j * part, (j + 1) * part)))

    @pl.when(i == nb)
    def _():
        finish_block(lambda g, j: None)


def _peer_gather(x1, idx, gate, u_tab, v_tab, ln_g, ln_b):
    t, d = x1.shape
    nb = t // PEER_G
    n_exp = u_tab.shape[0]
    uv = jnp.stack([u_tab, v_tab], axis=1).reshape(n_exp, 2, D_TILES, LANES)
    idx3 = idx.reshape(nb, 1, PEER_G * PEER_ROWS)
    cur = lambda i: (jnp.minimum(i, nb - 1), 0, 0)
    prv2 = lambda i: (jnp.maximum(i - 1, 0), 0)
    return pl.pallas_call(
        _peer_gather_kernel,
        grid=(nb + 1,),
        in_specs=[
            pl.BlockSpec((1, 1, PEER_G * PEER_ROWS), cur, memory_space=pltpu.SMEM),
            pl.BlockSpec((PEER_G, d), prv2),
            pl.BlockSpec((PEER_G, PEER_ROWS), prv2),
            pl.BlockSpec((D_TILES, LANES), lambda i: (0, 0)),
            pl.BlockSpec((D_TILES, LANES), lambda i: (0, 0)),
            pl.BlockSpec(memory_space=pl.ANY),
        ],
        out_specs=pl.BlockSpec((PEER_G, d), prv2),
        out_shape=jax.ShapeDtypeStruct((t, d), jnp.float32),
        scratch_shapes=[
            pltpu.VMEM((2, PEER_G, PEER_ROWS, 2, D_TILES, LANES), jnp.float32),
            pltpu.SemaphoreType.DMA((2,)),
        ],
        compiler_params=pltpu.CompilerParams(dimension_semantics=("arbitrary",),
                                             vmem_limit_bytes=PEER_VMEM_BYTES),
        name="peer_gather",
    )(idx3, x1, gate, ln_g.reshape(D_TILES, LANES), ln_b.reshape(D_TILES, LANES), uv)


def kernel(x, w_in, b_gate, conv_w, conv_b, rg_w_a, rg_b_a, rg_w_x, rg_b_x, rg_lambda, gm_ln_g, gm_ln_b, gm_ws, gm_bs, w_out, ln1_g, ln1_b, peer_wq, peer_sub_keys, peer_u, peer_v, ln2_g, ln2_b):
    bsz, s, d = x.shape
    assert w_in.shape[0] == 1, "single-layer stack"
    l = 0
    x1 = _token_mix(x, w_in[l], b_gate[l], conv_w[l], conv_b[l], rg_w_a[l], rg_b_a[l], rg_w_x[l], rg_b_x[l],
                    rg_lambda[l], gm_ln_g[l], gm_ln_b[l], gm_ws[l], gm_bs[l], w_out[l], ln1_g[l], ln1_b[l])
    keys = peer_sub_keys[l].reshape(2 * PEER_HEADS, N_KEYS, D_HALF).astype(jnp.bfloat16)
    idx, gate = _peer_route(x1, peer_wq[l].astype(jnp.bfloat16), keys)
    x2 = _peer_gather(x1, idx, gate, peer_u[l], peer_v[l], ln2_g[l], ln2_b[l])
    return x2.reshape(bsz, s, d)
```

```python
import jax
import jax.numpy as jnp
from jax import lax
from jax.experimental import pallas as pl
from jax.experimental.pallas import tpu as pltpu

D_MODEL = 1024
RNN_HEADS = 8
CONV_WIDTH = 4
LRU_C = 8.0
GMLP_HEADS = 8
CHUNK = 128
N_BRANCH = 2
PEER_HEADS = 8
N_KEYS = 128
D_HALF = 128
TOPK = 16
ALPHA = 2.0 ** 0.25
LN_EPS = 1e-5
NEG_INF = float("-inf")

LANES = 128
SUBLANES = 8
D_TILES = D_MODEL // LANES
PEER_ROWS = PEER_HEADS * TOPK

MIX_TM = 256
ROUTE_TM = 256
ROUTE_W = 256
PEER_G = 16
PEER_HALF = PEER_G // 2
MIX_VMEM_BYTES = 56 * 1024 * 1024
PEER_VMEM_BYTES = 48 * 1024 * 1024


def _ln_rows(y, g, b):
    mu = jnp.mean(y, axis=-1, keepdims=True)
    yc = y - mu
    var = jnp.mean(yc * yc, axis=-1, keepdims=True)
    return yc * lax.rsqrt(var + LN_EPS) * g + b


def _mix_kernel(x_ref, win_ref, bgate_ref, convw_ref, convb_ref, wax_ref, ba_ref, bx_ref, lam_ref,
                lng_ref, lnb_ref, ws_ref, bsf_ref, wout_ref, g1_ref, b1_ref,
                o_ref, xa_buf, h_ref):
    tm = x_ref.shape[0]
    d = D_MODEL
    hd = d // RNN_HEADS
    j = pl.program_id(1)

    @pl.when(j == 0)
    def _():
        xa_buf[0:SUBLANES, :] = jnp.zeros((SUBLANES, d), jnp.float32)
        h_ref[...] = jnp.zeros_like(h_ref)

    x = x_ref[...]
    xb = x.astype(jnp.bfloat16)

    def proj(k):
        return jnp.dot(xb, win_ref[:, k * d:(k + 1) * d], preferred_element_type=jnp.float32)

    xa_buf[SUBLANES:SUBLANES + tm, :] = proj(0)
    xc = convb_ref[...] + convw_ref[CONV_WIDTH - 1:CONV_WIDTH, :] * xa_buf[SUBLANES:SUBLANES + tm, :]
    for k in range(1, CONV_WIDTH):
        xc = xc + convw_ref[CONV_WIDTH - 1 - k:CONV_WIDTH - k, :] * xa_buf[SUBLANES - k:SUBLANES - k + tm, :]
    xa_buf[0:SUBLANES, :] = xa_buf[tm:tm + SUBLANES, :]

    xcb = xc.astype(jnp.bfloat16)
    pre = [jnp.dot(xcb[:, hh * hd:(hh + 1) * hd], wax_ref[hh], preferred_element_type=jnp.float32)
           for hh in range(RNN_HEADS)]
    r = jax.nn.sigmoid(jnp.concatenate([p[:, :hd] for p in pre], axis=1) + ba_ref[...])
    ig = jax.nn.sigmoid(jnp.concatenate([p[:, hd:] for p in pre], axis=1) + bx_ref[...])
    z = -lam_ref[...]
    softplus = jnp.maximum(z, 0.0) + jnp.log1p(jnp.exp(-jnp.abs(z)))
    log_a = (-LRU_C) * r * softplus
    a = jnp.exp(log_a)
    b = jnp.sqrt(1.0 - a * a) * (ig * xc)

    row = lax.broadcasted_iota(jnp.int32, (tm, d), 0)
    sh = 1
    while sh < tm:
        a_prev = jnp.where(row >= sh, pltpu.roll(a, sh, axis=0), 1.0)
        b_prev = jnp.where(row >= sh, pltpu.roll(b, sh, axis=0), 0.0)
        b = a * b_prev + b
        a = a * a_prev
        sh *= 2
    h = b + a * h_ref[...]
    h_ref[...] = h[tm - 1:tm, :]
    y = jax.nn.sigmoid(proj(4) + bgate_ref[0:1, :]) * (jax.nn.gelu(proj(1)) * h)

    vn = _ln_rows(jax.nn.gelu(proj(3)), lng_ref[...], lnb_ref[...]).astype(jnp.bfloat16)
    tri = (lax.broadcasted_iota(jnp.int32, (CHUNK, CHUNK), 0)
           >= lax.broadcasted_iota(jnp.int32, (CHUNK, CHUNK), 1))
    chunks = []
    for c in range(tm // CHUNK):
        heads = []
        for hh in range(GMLP_HEADS):
            w = jnp.where(tri, ws_ref[hh], 0.0).astype(jnp.bfloat16)
            heads.append(jnp.dot(w, vn[c * CHUNK:(c + 1) * CHUNK, hh * hd:(hh + 1) * hd],
                                 preferred_element_type=jnp.float32))
        chunks.append(jnp.concatenate(heads, axis=1) + bsf_ref[...])
    mixed = jnp.concatenate(chunks, axis=0)
    y = y + jax.nn.sigmoid(proj(5) + bgate_ref[1:2, :]) * (jax.nn.gelu(proj(2)) * mixed)

    o = jnp.dot(y.astype(jnp.bfloat16), wout_ref[...], preferred_element_type=jnp.float32)
    o_ref[...] = _ln_rows(ALPHA * x + o, g1_ref[...], b1_ref[...])


def _token_mix(x, w_in, b_gate, conv_w, conv_b, rg_w_a, rg_b_a, rg_w_x, rg_b_x, rg_lambda,
               gm_ln_g, gm_ln_b, gm_ws, gm_bs, w_out, ln1_g, ln1_b, tm=MIX_TM):
    bsz, s, d = x.shape
    nj = s // tm
    row = lambda v: v.reshape(1, d)
    wax = jnp.concatenate([rg_w_a, rg_w_x], axis=-1).astype(jnp.bfloat16)
    bs_full = jnp.repeat(gm_bs.T, d // GMLP_HEADS, axis=1)
    const2 = lambda shape: pl.BlockSpec(shape, lambda b, j: (0, 0))
    const3 = lambda shape: pl.BlockSpec(shape, lambda b, j: (0, 0, 0))
    return pl.pallas_call(
        _mix_kernel,
        grid=(bsz, nj),
        in_specs=[
            pl.BlockSpec((tm, d), lambda b, j: (b * nj + j, 0)),
            const2((d, w_in.shape[1])),
            const2((N_BRANCH, d)),
            const2((CONV_WIDTH, d)),
            const2((1, d)),
            const3(wax.shape),
            const2((1, d)), const2((1, d)), const2((1, d)),
            const2((1, d)), const2((1, d)),
            const3(gm_ws.shape),
            const2((CHUNK, d)),
            const2((d, d)),
            const2((1, d)), const2((1, d)),
        ],
        out_specs=pl.BlockSpec((tm, d), lambda b, j: (b * nj + j, 0)),
        out_shape=jax.ShapeDtypeStruct((bsz * s, d), jnp.float32),
        scratch_shapes=[pltpu.VMEM((tm + SUBLANES, d), jnp.float32),
                        pltpu.VMEM((1, d), jnp.float32)],
        compiler_params=pltpu.CompilerParams(dimension_semantics=("arbitrary", "arbitrary"),
                                             vmem_limit_bytes=MIX_VMEM_BYTES),
        name="token_mix",
    )(x.reshape(bsz * s, d), w_in.astype(jnp.bfloat16), b_gate, conv_w, row(conv_b), wax,
      row(rg_b_a), row(rg_b_x), row(rg_lambda), row(gm_ln_g), row(gm_ln_b), gm_ws, bs_full,
      w_out.astype(jnp.bfloat16), row(ln1_g), row(ln1_b))


def _extract_top(s, ids, rounds):
    rows = s.shape[0]
    pos = lax.broadcasted_iota(jnp.int32, s.shape, 0)
    vals, picked = [], []
    for _ in range(rounds):
        m = jnp.max(s, axis=0, keepdims=True)
        p = jnp.min(jnp.where(s == m, pos, rows), axis=0, keepdims=True)
        hit = pos == p
        vals.append(m)
        picked.append(p if ids is None else jnp.max(jnp.where(hit, ids, -1), axis=0, keepdims=True))
        s = jnp.where(hit, NEG_INF, s)
    return vals, picked


def _route_kernel(x_ref, wq_ref, keys_ref, idx_ref, gate_ref, s_ref, it_ref, gt_ref):
    tm = x_ref.shape[0]
    q = jnp.dot(x_ref[...].astype(jnp.bfloat16), wq_ref[...], preferred_element_type=jnp.float32)
    for j in range(2 * PEER_HEADS):
        s_ref[j] = lax.dot_general(keys_ref[j], q[:, j * D_HALF:(j + 1) * D_HALF].astype(jnp.bfloat16),
                                   (((1,), (1,)), ((), ())), preferred_element_type=jnp.float32)

    sub = lax.broadcasted_iota(jnp.int32, (SUBLANES, ROUTE_W), 0)

    def tile(tl, carry):
        c0 = pl.multiple_of(tl * ROUTE_W, ROUTE_W)

        def head(h, c):
            v1, i1 = _extract_top(s_ref[2 * h, :, pl.ds(c0, ROUTE_W)], None, TOPK)
            v2, i2 = _extract_top(s_ref[2 * h + 1, :, pl.ds(c0, ROUTE_W)], None, TOPK)
            v2lo = jnp.concatenate(v2[:SUBLANES], axis=0)
            v2hi = jnp.concatenate(v2[SUBLANES:], axis=0)
            i2lo = jnp.concatenate(i2[:SUBLANES], axis=0)
            i2hi = jnp.concatenate(i2[SUBLANES:], axis=0)
            cs = [v1[0] + v2lo, v1[0] + v2hi]
            ci = [i1[0] * N_KEYS + i2lo, i1[0] * N_KEYS + i2hi]
            for a in range(1, TOPK):
                keep = sub < (TOPK // (a + 1))
                cs.append(jnp.where(keep, v1[a] + v2lo, NEG_INF))
                ci.append(i1[a] * N_KEYS + i2lo)
            tv, te = _extract_top(jnp.concatenate(cs, axis=0), jnp.concatenate(ci, axis=0), TOPK)
            tv = jnp.concatenate(tv, axis=0)
            ex = jnp.exp(tv - tv[0:1])
            g = ex / jnp.sum(ex, axis=0, keepdims=True)
            r0 = pl.multiple_of(h * TOPK, TOPK)
            gt_ref[pl.ds(r0, TOPK), :] = g
            it_ref[pl.ds(r0, TOPK), :] = jnp.concatenate(te, axis=0)
            return c

        lax.fori_loop(0, PEER_HEADS, head, 0)
        gate_ref[pl.ds(c0, ROUTE_W), :] = gt_ref[...].T
        idx_ref[pl.ds(c0, ROUTE_W), :] = it_ref[...].T
        return carry

    lax.fori_loop(0, tm // ROUTE_W, tile, 0)


def _peer_route(x1, wq_bf16, keys_bf16, tm=ROUTE_TM):
    t, d = x1.shape
    nq = wq_bf16.shape[1]
    return pl.pallas_call(
        _route_kernel,
        grid=(t // tm,),
        in_specs=[pl.BlockSpec((tm, d), lambda i: (i, 0)),
                  pl.BlockSpec((d, nq), lambda i: (0, 0)),
                  pl.BlockSpec((2 * PEER_HEADS, N_KEYS, D_HALF), lambda i: (0, 0, 0))],
        out_specs=[pl.BlockSpec((tm, PEER_ROWS), lambda i: (i, 0)),
                   pl.BlockSpec((tm, PEER_ROWS), lambda i: (i, 0))],
        out_shape=[jax.ShapeDtypeStruct((t, PEER_ROWS), jnp.int32),
                   jax.ShapeDtypeStruct((t, PEER_ROWS), jnp.float32)],
        scratch_shapes=[pltpu.VMEM((2 * PEER_HEADS, N_KEYS, tm), jnp.float32),
                        pltpu.VMEM((PEER_ROWS, ROUTE_W), jnp.int32),
                        pltpu.VMEM((PEER_ROWS, ROUTE_W), jnp.float32)],
        compiler_params=pltpu.CompilerParams(dimension_semantics=("arbitrary",)),
        name="peer_route",
    )(x1, wq_bf16, keys_bf16)


def _sublane_sums(tiles, sub):
    step = SUBLANES // 2
    while step >= 1:
        low = (sub & step) == 0
        nxt = []
        for j in range(len(tiles) // 2):
            a, b = tiles[j], tiles[j + len(tiles) // 2]
            nxt.append(jnp.where(low, a, pltpu.roll(b, step, axis=0))
                       + jnp.where(low, pltpu.roll(a, SUBLANES - step, axis=0), b))
        tiles = nxt
        step //= 2
    return tiles[0]


def _peer_gather_kernel(idx_ref, x_ref, gate_ref, g2_ref, b2_ref, uv_hbm, o_ref, buf, sem):
    i = pl.program_id(0)
    nb = pl.num_programs(0) - 1
    slot = lax.rem(i, 2)
    prev = 1 - slot

    def start_rows(g, ks):
        for k in ks:
            e = idx_ref[0, 0, g * PEER_ROWS + k]
            pltpu.make_async_copy(uv_hbm.at[e], buf.at[slot, g, k],
                                  sem.at[slot, g // PEER_HALF]).start(priority=k % 2)

    groups = PEER_ROWS // SUBLANES

    def tile_of_row(g):
        return jnp.concatenate([x_ref[g:g + 1, c * LANES:(c + 1) * LANES] for c in range(D_TILES)], axis=0)

    def dots(g, x8, sub, fetch):
        parts = []
        for q in range(groups):
            prods = [buf[prev, g, q * SUBLANES + j, 0] * x8 for j in range(SUBLANES)]
            fetch(g, 2 * q)
            parts.append(_sublane_sums(prods, sub))
        return jnp.concatenate(parts, axis=0)

    def combine(g, x8, act, fetch):
        out = None
        for k in range(PEER_ROWS):
            term = buf[prev, g, k, 1] * act[k:k + 1, :]
            out = term if out is None else out + term
            if k % SUBLANES == SUBLANES - 1:
                fetch(g, 2 * (k // SUBLANES) + 1)
        p = ALPHA * x8 + out
        mu = jnp.mean(p, keepdims=True)
        pc = p - mu
        var = jnp.mean(pc * pc, keepdims=True)
        y8 = pc * lax.rsqrt(var + LN_EPS) * g2_ref[...] + b2_ref[...]
        for c in range(D_TILES):
            o_ref[g:g + 1, c * LANES:(c + 1) * LANES] = y8[c:c + 1, :]

    def finish_block(fetch):
        def arrived(half):
            rows = buf.at[prev, pl.ds(half * PEER_HALF, PEER_HALF)]
            pltpu.make_async_copy(rows, rows, sem.at[prev, half]).wait()

        gate_t = gate_ref[...].T
        lane = lax.broadcasted_iota(jnp.int32, gate_t.shape, 1)
        sub = lax.broadcasted_iota(jnp.int32, (SUBLANES, LANES), 0)
        arrived(0)
        x8 = tile_of_row(0)
        acc = dots(0, x8, sub, fetch)
        for g in range(PEER_G):
            h = jnp.sum(acc, axis=1, keepdims=True)
            gcol = jnp.sum(jnp.where(lane == g, gate_t, 0.0), axis=1, keepdims=True)
            act = jax.nn.gelu(h) * gcol
            x8_cur = x8
            if g + 1 < PEER_G:
                if (g + 1) % PEER_HALF == 0:
                    arrived((g + 1) // PEER_HALF)
                x8 = tile_of_row(g + 1)
                acc = dots(g + 1, x8, sub, fetch)
            combine(g, x8_cur, act, fetch)

    part = SUBLANES // 2

    @pl.when(i == 0)
    def _():
        def first(g, c):
            start_rows(g, range(PEER_ROWS))
            return c
        lax.fori_loop(0, PEER_G, first, 0)

    @pl.when(jnp.logical_and(i > 0, i < nb))
    def _():
        finish_block(lambda g, j: start_rows(g, range(j * part, (j + 1) * part)))

    @pl.when(i == nb)
    def _():
        finish_block(lambda g, j: None)


def _peer_gather(x1, idx, gate, u_tab, v_tab, ln_g, ln_b):
    t, d = x1.shape
    nb = t // PEER_G
    n_exp = u_tab.shape[0]
    uv = jnp.stack([u_tab, v_tab], axis=1).reshape(n_exp, 2, D_TILES, LANES)
    idx3 = idx.reshape(nb, 1, PEER_G * PEER_ROWS)
    cur = lambda i: (jnp.minimum(i, nb - 1), 0, 0)
    prv2 = lambda i: (jnp.maximum(i - 1, 0), 0)
    return pl.pallas_call(
        _peer_gather_kernel,
        grid=(nb + 1,),
        in_specs=[
            pl.BlockSpec((1, 1, PEER_G * PEER_ROWS), cur, memory_space=pltpu.SMEM),
            pl.BlockSpec((PEER_G, d), prv2),
            pl.BlockSpec((PEER_G, PEER_ROWS), prv2),
            pl.BlockSpec((D_TILES, LANES), lambda i: (0, 0)),
            pl.BlockSpec((D_TILES, LANES), lambda i: (0, 0)),
            pl.BlockSpec(memory_space=pl.ANY),
        ],
        out_specs=pl.BlockSpec((PEER_G, d), prv2),
        out_shape=jax.ShapeDtypeStruct((t, d), jnp.float32),
        scratch_shapes=[
            pltpu.VMEM((2, PEER_G, PEER_ROWS, 2, D_TILES, LANES), jnp.float32),
            pltpu.SemaphoreType.DMA((2, PEER_G // PEER_HALF)),
        ],
        compiler_params=pltpu.CompilerParams(dimension_semantics=("arbitrary",),
                                             vmem_limit_bytes=PEER_VMEM_BYTES),
        name="peer_gather",
    )(idx3, x1, gate, ln_g.reshape(D_TILES, LANES), ln_b.reshape(D_TILES, LANES), uv)


def kernel(x, w_in, b_gate, conv_w, conv_b, rg_w_a, rg_b_a, rg_w_x, rg_b_x, rg_lambda, gm_ln_g, gm_ln_b, gm_ws, gm_bs, w_out, ln1_g, ln1_b, peer_wq, peer_sub_keys, peer_u, peer_v, ln2_g, ln2_b):
    bsz, s, d = x.shape
    assert w_in.shape[0] == 1, "single-layer stack"
    l = 0
    x1 = _token_mix(x, w_in[l], b_gate[l], conv_w[l], conv_b[l], rg_w_a[l], rg_b_a[l], rg_w_x[l], rg_b_x[l],
                    rg_lambda[l], gm_ln_g[l], gm_ln_b[l], gm_ws[l], gm_bs[l], w_out[l], ln1_g[l], ln1_b[l])
    keys = peer_sub_keys[l].reshape(2 * PEER_HEADS, N_KEYS, D_HALF).astype(jnp.bfloat16)
    idx, gate = _peer_route(x1, peer_wq[l].astype(jnp.bfloat16), keys)
    x2 = _peer_gather(x1, idx, gate, peer_u[l], peer_v[l], ln2_g[l], ln2_b[l])
    return x2.reshape(bsz, s, d)
```

```python
import jax
import jax.numpy as jnp
from jax import lax
from jax.experimental import pallas as pl
from jax.experimental.pallas import tpu as pltpu

D_MODEL = 1024
RNN_HEADS = 8
CONV_WIDTH = 4
LRU_C = 8.0
GMLP_HEADS = 8
CHUNK = 128
N_BRANCH = 2
PEER_HEADS = 8
N_KEYS = 128
D_HALF = 128
TOPK = 16
ALPHA = 2.0 ** 0.25
LN_EPS = 1e-5
NEG_INF = float("-inf")

LANES = 128
SUBLANES = 8
D_TILES = D_MODEL // LANES
PEER_ROWS = PEER_HEADS * TOPK

MIX_TM = 512
ROUTE_TM = 512
ROUTE_W = 256
PEER_G = 16
PEER_HALF = PEER_G // 2
MIX_VMEM_BYTES = 56 * 1024 * 1024
PEER_VMEM_BYTES = 48 * 1024 * 1024


def _ln_rows(y, g, b):
    mu = jnp.mean(y, axis=-1, keepdims=True)
    yc = y - mu
    var = jnp.mean(yc * yc, axis=-1, keepdims=True)
    return yc * lax.rsqrt(var + LN_EPS) * g + b


def _mix_kernel(x_ref, win_ref, bgate_ref, convw_ref, convb_ref, wax_ref, ba_ref, bx_ref, lam_ref,
                lng_ref, lnb_ref, ws_ref, bsf_ref, wout_ref, g1_ref, b1_ref,
                o_ref, xa_buf, h_ref):
    tm = x_ref.shape[0]
    d = D_MODEL
    hd = d // RNN_HEADS
    j = pl.program_id(1)

    @pl.when(j == 0)
    def _():
        xa_buf[0:SUBLANES, :] = jnp.zeros((SUBLANES, d), jnp.float32)
        h_ref[...] = jnp.zeros_like(h_ref)

    x = x_ref[...]
    xb = x.astype(jnp.bfloat16)

    def proj(k):
        return jnp.dot(xb, win_ref[:, k * d:(k + 1) * d], preferred_element_type=jnp.float32)

    xa_buf[SUBLANES:SUBLANES + tm, :] = proj(0)
    xc = convb_ref[...] + convw_ref[CONV_WIDTH - 1:CONV_WIDTH, :] * xa_buf[SUBLANES:SUBLANES + tm, :]
    for k in range(1, CONV_WIDTH):
        xc = xc + convw_ref[CONV_WIDTH - 1 - k:CONV_WIDTH - k, :] * xa_buf[SUBLANES - k:SUBLANES - k + tm, :]
    xa_buf[0:SUBLANES, :] = xa_buf[tm:tm + SUBLANES, :]

    xcb = xc.astype(jnp.bfloat16)
    pre = [jnp.dot(xcb[:, hh * hd:(hh + 1) * hd], wax_ref[hh], preferred_element_type=jnp.float32)
           for hh in range(RNN_HEADS)]
    r = jax.nn.sigmoid(jnp.concatenate([p[:, :hd] for p in pre], axis=1) + ba_ref[...])
    ig = jax.nn.sigmoid(jnp.concatenate([p[:, hd:] for p in pre], axis=1) + bx_ref[...])
    z = -lam_ref[...]
    softplus = jnp.maximum(z, 0.0) + jnp.log1p(jnp.exp(-jnp.abs(z)))
    log_a = (-LRU_C) * r * softplus
    a = jnp.exp(log_a)
    b = jnp.sqrt(1.0 - a * a) * (ig * xc)

    row = lax.broadcasted_iota(jnp.int32, (tm, d), 0)
    sh = 1
    while sh < tm:
        a_prev = jnp.where(row >= sh, pltpu.roll(a, sh, axis=0), 1.0)
        b_prev = jnp.where(row >= sh, pltpu.roll(b, sh, axis=0), 0.0)
        b = a * b_prev + b
        a = a * a_prev
        sh *= 2
    h = b + a * h_ref[...]
    h_ref[...] = h[tm - 1:tm, :]
    y = jax.nn.sigmoid(proj(4) + bgate_ref[0:1, :]) * (jax.nn.gelu(proj(1)) * h)

    vn = _ln_rows(jax.nn.gelu(proj(3)), lng_ref[...], lnb_ref[...]).astype(jnp.bfloat16)
    tri = (lax.broadcasted_iota(jnp.int32, (CHUNK, CHUNK), 0)
           >= lax.broadcasted_iota(jnp.int32, (CHUNK, CHUNK), 1))
    chunks = []
    for c in range(tm // CHUNK):
        heads = []
        for hh in range(GMLP_HEADS):
            w = jnp.where(tri, ws_ref[hh], 0.0).astype(jnp.bfloat16)
            heads.append(jnp.dot(w, vn[c * CHUNK:(c + 1) * CHUNK, hh * hd:(hh + 1) * hd],
                                 preferred_element_type=jnp.float32))
        chunks.append(jnp.concatenate(heads, axis=1) + bsf_ref[...])
    mixed = jnp.concatenate(chunks, axis=0)
    y = y + jax.nn.sigmoid(proj(5) + bgate_ref[1:2, :]) * (jax.nn.gelu(proj(2)) * mixed)

    o = jnp.dot(y.astype(jnp.bfloat16), wout_ref[...], preferred_element_type=jnp.float32)
    o_ref[...] = _ln_rows(ALPHA * x + o, g1_ref[...], b1_ref[...])


def _token_mix(x, w_in, b_gate, conv_w, conv_b, rg_w_a, rg_b_a, rg_w_x, rg_b_x, rg_lambda,
               gm_ln_g, gm_ln_b, gm_ws, gm_bs, w_out, ln1_g, ln1_b, tm=MIX_TM):
    bsz, s, d = x.shape
    nj = s // tm
    row = lambda v: v.reshape(1, d)
    wax = jnp.concatenate([rg_w_a, rg_w_x], axis=-1).astype(jnp.bfloat16)
    bs_full = jnp.repeat(gm_bs.T, d // GMLP_HEADS, axis=1)
    const2 = lambda shape: pl.BlockSpec(shape, lambda b, j: (0, 0))
    const3 = lambda shape: pl.BlockSpec(shape, lambda b, j: (0, 0, 0))
    return pl.pallas_call(
        _mix_kernel,
        grid=(bsz, nj),
        in_specs=[
            pl.BlockSpec((tm, d), lambda b, j: (b * nj + j, 0)),
            const2((d, w_in.shape[1])),
            const2((N_BRANCH, d)),
            const2((CONV_WIDTH, d)),
            const2((1, d)),
            const3(wax.shape),
            const2((1, d)), const2((1, d)), const2((1, d)),
            const2((1, d)), const2((1, d)),
            const3(gm_ws.shape),
            const2((CHUNK, d)),
            const2((d, d)),
            const2((1, d)), const2((1, d)),
        ],
        out_specs=pl.BlockSpec((tm, d), lambda b, j: (b * nj + j, 0)),
        out_shape=jax.ShapeDtypeStruct((bsz * s, d), jnp.float32),
        scratch_shapes=[pltpu.VMEM((tm + SUBLANES, d), jnp.float32),
                        pltpu.VMEM((1, d), jnp.float32)],
        compiler_params=pltpu.CompilerParams(dimension_semantics=("arbitrary", "arbitrary"),
                                             vmem_limit_bytes=MIX_VMEM_BYTES),
        name="token_mix",
    )(x.reshape(bsz * s, d), w_in.astype(jnp.bfloat16), b_gate, conv_w, row(conv_b), wax,
      row(rg_b_a), row(rg_b_x), row(rg_lambda), row(gm_ln_g), row(gm_ln_b), gm_ws, bs_full,
      w_out.astype(jnp.bfloat16), row(ln1_g), row(ln1_b))


def _sub_max(x):
    return jnp.max(x, axis=0, keepdims=True)


def _sub_min(x):
    return jnp.min(x, axis=0, keepdims=True)


def _top_of_rows(s, sub):
    depth = N_KEYS // SUBLANES
    lv = [s[j * SUBLANES:(j + 1) * SUBLANES] for j in range(depth)]
    ix = [sub + j * SUBLANES for j in range(depth)]
    for phase in range(depth):
        for j in range(phase % 2, depth - 1, 2):
            swap = lv[j + 1] > lv[j]
            lv[j], lv[j + 1] = jnp.where(swap, lv[j + 1], lv[j]), jnp.where(swap, lv[j], lv[j + 1])
            ix[j], ix[j + 1] = jnp.where(swap, ix[j + 1], ix[j]), jnp.where(swap, ix[j], ix[j + 1])
    vals, picked = [], []
    for t in range(TOPK):
        m = _sub_max(lv[0])
        p = _sub_min(jnp.where(lv[0] == m, ix[0], N_KEYS))
        hit = ix[0] == p
        vals.append(m)
        picked.append(p)
        for j in range(depth - 1 - t):
            lv[j] = jnp.where(hit, lv[j + 1], lv[j])
            ix[j] = jnp.where(hit, ix[j + 1], ix[j])
    return vals, picked


def _top_of_sums(v1, i1, v2, i2, sub):
    v1a = jnp.concatenate(v1[:SUBLANES], axis=0)
    v1b = jnp.concatenate(v1[SUBLANES:], axis=0)
    e1a = jnp.concatenate(i1[:SUBLANES], axis=0) * N_KEYS
    e1b = jnp.concatenate(i1[SUBLANES:], axis=0) * N_KEYS
    lv = [jnp.where(sub < TOPK // (b + 1), v1a + v2[b], NEG_INF) for b in range(TOPK)]
    ev = [e1a + i2[b] for b in range(TOPK)]
    lb = v1b + v2[0]
    eb = e1b + i2[0]
    col_a, col_b = sub, sub + SUBLANES
    vals, picked = [], []
    for t in range(TOPK):
        m = jnp.maximum(_sub_max(lv[0]), _sub_max(lb))
        p = jnp.minimum(_sub_min(jnp.where(lv[0] == m, col_a, TOPK)), _sub_min(jnp.where(lb == m, col_b, TOPK)))
        hit_a = col_a == p
        hit_b = col_b == p
        vals.append(m)
        picked.append(jnp.maximum(_sub_max(jnp.where(hit_a, ev[0], -1)), _sub_max(jnp.where(hit_b, eb, -1))))
        for j in range(TOPK - 1 - t):
            lv[j] = jnp.where(hit_a, lv[j + 1], lv[j])
            ev[j] = jnp.where(hit_a, ev[j + 1], ev[j])
        lb = jnp.where(hit_b, NEG_INF, lb)
    return vals, picked


def _route_kernel(x_ref, wq_ref, keys_ref, idx_ref, gate_ref, s_ref, it_ref, gt_ref):
    tm = x_ref.shape[0]
    q = jnp.dot(x_ref[...].astype(jnp.bfloat16), wq_ref[...], preferred_element_type=jnp.float32)
    for j in range(2 * PEER_HEADS):
        s_ref[j] = lax.dot_general(keys_ref[j], q[:, j * D_HALF:(j + 1) * D_HALF].astype(jnp.bfloat16),
                                   (((1,), (1,)), ((), ())), preferred_element_type=jnp.float32)

    sub = lax.broadcasted_iota(jnp.int32, (SUBLANES, ROUTE_W), 0)

    def tile(tl, carry):
        c0 = pl.multiple_of(tl * ROUTE_W, ROUTE_W)

        def head(h, c):
            v1, i1 = _top_of_rows(s_ref[2 * h, :, pl.ds(c0, ROUTE_W)], sub)
            v2, i2 = _top_of_rows(s_ref[2 * h + 1, :, pl.ds(c0, ROUTE_W)], sub)
            tv, te = _top_of_sums(v1, i1, v2, i2, sub)
            tv = jnp.concatenate(tv, axis=0)
            ex = jnp.exp(tv - tv[0:1])
            g = ex / jnp.sum(ex, axis=0, keepdims=True)
            r0 = pl.multiple_of(h * TOPK, TOPK)
            gt_ref[pl.ds(r0, TOPK), :] = g
            it_ref[pl.ds(r0, TOPK), :] = jnp.concatenate(te, axis=0)
            return c

        lax.fori_loop(0, PEER_HEADS, head, 0)
        gate_ref[pl.ds(c0, ROUTE_W), :] = gt_ref[...].T
        idx_ref[pl.ds(c0, ROUTE_W), :] = it_ref[...].T
        return carry

    lax.fori_loop(0, tm // ROUTE_W, tile, 0)


def _peer_route(x1, wq_bf16, keys_bf16, tm=ROUTE_TM):
    t, d = x1.shape
    nq = wq_bf16.shape[1]
    return pl.pallas_call(
        _route_kernel,
        grid=(t // tm,),
        in_specs=[pl.BlockSpec((tm, d), lambda i: (i, 0)),
                  pl.BlockSpec((d, nq), lambda i: (0, 0)),
                  pl.BlockSpec((2 * PEER_HEADS, N_KEYS, D_HALF), lambda i: (0, 0, 0))],
        out_specs=[pl.BlockSpec((tm, PEER_ROWS), lambda i: (i, 0)),
                   pl.BlockSpec((tm, PEER_ROWS), lambda i: (i, 0))],
        out_shape=[jax.ShapeDtypeStruct((t, PEER_ROWS), jnp.int32),
                   jax.ShapeDtypeStruct((t, PEER_ROWS), jnp.float32)],
        scratch_shapes=[pltpu.VMEM((2 * PEER_HEADS, N_KEYS, tm), jnp.float32),
                        pltpu.VMEM((PEER_ROWS, ROUTE_W), jnp.int32),
                        pltpu.VMEM((PEER_ROWS, ROUTE_W), jnp.float32)],
        compiler_params=pltpu.CompilerParams(dimension_semantics=("arbitrary",)),
        name="peer_route",
    )(x1, wq_bf16, keys_bf16)


def _sublane_sums(tiles, sub):
    step = SUBLANES // 2
    while step >= 1:
        low = (sub & step) == 0
        nxt = []
        for j in range(len(tiles) // 2):
            a, b = tiles[j], tiles[j + len(tiles) // 2]
            nxt.append(jnp.where(low, a, pltpu.roll(b, step, axis=0))
                       + jnp.where(low, pltpu.roll(a, SUBLANES - step, axis=0), b))
        tiles = nxt
        step //= 2
    return tiles[0]


def _peer_gather_kernel(idx_ref, x_ref, gate_ref, g2_ref, b2_ref, uv_hbm, o_ref, buf, sem):
    i = pl.program_id(0)
    nb = pl.num_programs(0) - 1
    slot = lax.rem(i, 2)
    prev = 1 - slot

    def start_rows(g, ks):
        for k in ks:
            e = idx_ref[0, 0, g * PEER_ROWS + k]
            pltpu.make_async_copy(uv_hbm.at[e], buf.at[slot, g, k],
                                  sem.at[slot, g // PEER_HALF]).start(priority=k % 2)

    groups = PEER_ROWS // SUBLANES

    def tile_of_row(g):
        return jnp.concatenate([x_ref[g:g + 1, c * LANES:(c + 1) * LANES] for c in range(D_TILES)], axis=0)

    def dots(g, x8, sub, fetch):
        parts = []
        for q in range(groups):
            prods = [buf[prev, g, q * SUBLANES + j, 0] * x8 for j in range(SUBLANES)]
            fetch(g, 2 * q)
            parts.append(_sublane_sums(prods, sub))
        return jnp.concatenate(parts, axis=0)

    def combine(g, x8, act, fetch):
        out = None
        for k in range(PEER_ROWS):
            term = buf[prev, g, k, 1] * act[k:k + 1, :]
            out = term if out is None else out + term
            if k % SUBLANES == SUBLANES - 1:
                fetch(g, 2 * (k // SUBLANES) + 1)
        p = ALPHA * x8 + out
        mu = jnp.mean(p, keepdims=True)
        pc = p - mu
        var = jnp.mean(pc * pc, keepdims=True)
        y8 = pc * lax.rsqrt(var + LN_EPS) * g2_ref[...] + b2_ref[...]
        for c in range(D_TILES):
            o_ref[g:g + 1, c * LANES:(c + 1) * LANES] = y8[c:c + 1, :]

    def finish_block(fetch):
        def arrived(half):
            rows = buf.at[prev, pl.ds(half * PEER_HALF, PEER_HALF)]
            pltpu.make_async_copy(rows, rows, sem.at[prev, half]).wait()

        gate_t = gate_ref[...].T
        lane = lax.broadcasted_iota(jnp.int32, gate_t.shape, 1)
        sub = lax.broadcasted_iota(jnp.int32, (SUBLANES, LANES), 0)
        arrived(0)
        x8 = tile_of_row(0)
        acc = dots(0, x8, sub, fetch)
        for g in range(PEER_G):
            h = jnp.sum(acc, axis=1, keepdims=True)
            gcol = jnp.sum(jnp.where(lane == g, gate_t, 0.0), axis=1, keepdims=True)
            act = jax.nn.gelu(h) * gcol
            x8_cur = x8
            if g + 1 < PEER_G:
                if (g + 1) % PEER_HALF == 0:
                    arrived((g + 1) // PEER_HALF)
                x8 = tile_of_row(g + 1)
                acc = dots(g + 1, x8, sub, fetch)
            combine(g, x8_cur, act, fetch)

    part = SUBLANES // 2

    @pl.when(i == 0)
    def _():
        def first(g, c):
            start_rows(g, range(PEER_ROWS))
            return c
        lax.fori_loop(0, PEER_G, first, 0)

    @pl.when(jnp.logical_and(i > 0, i < nb))
    def _():
        finish_block(lambda g, j: start_rows(g, range(j * part, (j + 1) * part)))

    @pl.when(i == nb)
    def _():
        finish_block(lambda g, j: None)


def _peer_gather(x1, idx, gate, u_tab, v_tab, ln_g, ln_b):
    t, d = x1.shape
    nb = t // PEER_G
    n_exp = u_tab.shape[0]
    uv = jnp.stack([u_tab, v_tab], axis=1).reshape(n_exp, 2, D_TILES, LANES)
    idx3 = idx.reshape(nb, 1, PEER_G * PEER_ROWS)
    cur = lambda i: (jnp.minimum(i, nb - 1), 0, 0)
    prv2 = lambda i: (jnp.maximum(i - 1, 0), 0)
    return pl.pallas_call(
        _peer_gather_kernel,
        grid=(nb + 1,),
        in_specs=[
            pl.BlockSpec((1, 1, PEER_G * PEER_ROWS), cur, memory_space=pltpu.SMEM),
            pl.BlockSpec((PEER_G, d), prv2),
            pl.BlockSpec((PEER_G, PEER_ROWS), prv2),
            pl.BlockSpec((D_TILES, LANES), lambda i: (0, 0)),
            pl.BlockSpec((D_TILES, LANES), lambda i: (0, 0)),
            pl.BlockSpec(memory_space=pl.ANY),
        ],
        out_specs=pl.BlockSpec((PEER_G, d), prv2),
        out_shape=jax.ShapeDtypeStruct((t, d), jnp.float32),
        scratch_shapes=[
            pltpu.VMEM((2, PEER_G, PEER_ROWS, 2, D_TILES, LANES), jnp.float32),
            pltpu.SemaphoreType.DMA((2, PEER_G // PEER_HALF)),
        ],
        compiler_params=pltpu.CompilerParams(dimension_semantics=("arbitrary",),
                                             vmem_limit_bytes=PEER_VMEM_BYTES),
        name="peer_gather",
    )(idx3, x1, gate, ln_g.reshape(D_TILES, LANES), ln_b.reshape(D_TILES, LANES), uv)


def kernel(x, w_in, b_gate, conv_w, conv_b, rg_w_a, rg_b_a, rg_w_x, rg_b_x, rg_lambda, gm_ln_g, gm_ln_b, gm_ws, gm_bs, w_out, ln1_g, ln1_b, peer_wq, peer_sub_keys, peer_u, peer_v, ln2_g, ln2_b):
    bsz, s, d = x.shape
    assert w_in.shape[0] == 1, "single-layer stack"
    l = 0
    x1 = _token_mix(x, w_in[l], b_gate[l], conv_w[l], conv_b[l], rg_w_a[l], rg_b_a[l], rg_w_x[l], rg_b_x[l],
                    rg_lambda[l], gm_ln_g[l], gm_ln_b[l], gm_ws[l], gm_bs[l], w_out[l], ln1_g[l], ln1_b[l])
    keys = peer_sub_keys[l].reshape(2 * PEER_HEADS, N_KEYS, D_HALF).astype(jnp.bfloat16)
    idx, gate = _peer_route(x1, peer_wq[l].astype(jnp.bfloat16), keys)
    x2 = _peer_gather(x1, idx, gate, peer_u[l], peer_v[l], ln2_g[l], ln2_b[l])
    return x2.reshape(bsz, s, d)
```

```python
import jax
import jax.numpy as jnp
from jax import lax
from jax.experimental import pallas as pl
from jax.experimental.pallas import tpu as pltpu

D_MODEL = 1024
RNN_HEADS = 8
CONV_WIDTH = 4
LRU_C = 8.0
GMLP_HEADS = 8
CHUNK = 128
N_BRANCH = 2
PEER_HEADS = 8
N_KEYS = 128
D_HALF = 128
TOPK = 16
ALPHA = 2.0 ** 0.25
LN_EPS = 1e-5
NEG_INF = float("-inf")

LANES = 128
SUBLANES = 8
D_TILES = D_MODEL // LANES
PEER_ROWS = PEER_HEADS * TOPK

MIX_TM = 512
ROUTE_TM = 512
ROUTE_W = 256
PEER_G = 16
PEER_HALF = PEER_G // 2
MIX_VMEM_BYTES = 56 * 1024 * 1024
PEER_VMEM_BYTES = 48 * 1024 * 1024


def _ln_rows(y, g, b):
    mu = jnp.mean(y, axis=-1, keepdims=True)
    yc = y - mu
    var = jnp.mean(yc * yc, axis=-1, keepdims=True)
    return yc * lax.rsqrt(var + LN_EPS) * g + b


def _mix_kernel(x_ref, win_ref, bgate_ref, convw_ref, convb_ref, wax_ref, ba_ref, bx_ref, lam_ref,
                lng_ref, lnb_ref, ws_ref, bsf_ref, wout_ref, g1_ref, b1_ref,
                o_ref, xa_buf, h_ref):
    tm = x_ref.shape[0]
    d = D_MODEL
    hd = d // RNN_HEADS
    j = pl.program_id(1)

    @pl.when(j == 0)
    def _():
        xa_buf[0:SUBLANES, :] = jnp.zeros((SUBLANES, d), jnp.float32)
        h_ref[...] = jnp.zeros_like(h_ref)

    x = x_ref[...]
    xb = x.astype(jnp.bfloat16)

    def proj(k):
        return jnp.dot(xb, win_ref[:, k * d:(k + 1) * d], preferred_element_type=jnp.float32)

    xa_buf[SUBLANES:SUBLANES + tm, :] = proj(0)
    xc = convb_ref[...] + convw_ref[CONV_WIDTH - 1:CONV_WIDTH, :] * xa_buf[SUBLANES:SUBLANES + tm, :]
    for k in range(1, CONV_WIDTH):
        xc = xc + convw_ref[CONV_WIDTH - 1 - k:CONV_WIDTH - k, :] * xa_buf[SUBLANES - k:SUBLANES - k + tm, :]
    xa_buf[0:SUBLANES, :] = xa_buf[tm:tm + SUBLANES, :]

    xcb = xc.astype(jnp.bfloat16)
    pre = [jnp.dot(xcb[:, hh * hd:(hh + 1) * hd], wax_ref[hh], preferred_element_type=jnp.float32)
           for hh in range(RNN_HEADS)]
    r = jax.nn.sigmoid(jnp.concatenate([p[:, :hd] for p in pre], axis=1) + ba_ref[...])
    ig = jax.nn.sigmoid(jnp.concatenate([p[:, hd:] for p in pre], axis=1) + bx_ref[...])
    z = -lam_ref[...]
    softplus = jnp.maximum(z, 0.0) + jnp.log1p(jnp.exp(-jnp.abs(z)))
    log_a = (-LRU_C) * r * softplus
    a = jnp.exp(log_a)
    b = jnp.sqrt(1.0 - a * a) * (ig * xc)

    sub = lax.broadcasted_iota(jnp.int32, (SUBLANES, d), 0)
    carry = h_ref[...]
    hs = []
    for r in range(tm // SUBLANES):
        ag = a[r * SUBLANES:(r + 1) * SUBLANES]
        bg = b[r * SUBLANES:(r + 1) * SUBLANES]
        sh = 1
        while sh < SUBLANES:
            a_prev = jnp.where(sub >= sh, pltpu.roll(ag, sh, axis=0), 1.0)
            b_prev = jnp.where(sub >= sh, pltpu.roll(bg, sh, axis=0), 0.0)
            bg = ag * b_prev + bg
            ag = ag * a_prev
            sh *= 2
        hg = bg + ag * carry
        carry = hg[SUBLANES - 1:SUBLANES, :]
        hs.append(hg)
    h = jnp.concatenate(hs, axis=0)
    h_ref[...] = carry
    y = jax.nn.sigmoid(proj(4) + bgate_ref[0:1, :]) * (jax.nn.gelu(proj(1)) * h)

    vn = _ln_rows(jax.nn.gelu(proj(3)), lng_ref[...], lnb_ref[...]).astype(jnp.bfloat16)
    tri = (lax.broadcasted_iota(jnp.int32, (CHUNK, CHUNK), 0)
           >= lax.broadcasted_iota(jnp.int32, (CHUNK, CHUNK), 1))
    chunks = []
    for c in range(tm // CHUNK):
        heads = []
        for hh in range(GMLP_HEADS):
            w = jnp.where(tri, ws_ref[hh], 0.0).astype(jnp.bfloat16)
            heads.append(jnp.dot(w, vn[c * CHUNK:(c + 1) * CHUNK, hh * hd:(hh + 1) * hd],
                                 preferred_element_type=jnp.float32))
        chunks.append(jnp.concatenate(heads, axis=1) + bsf_ref[...])
    mixed = jnp.concatenate(chunks, axis=0)
    y = y + jax.nn.sigmoid(proj(5) + bgate_ref[1:2, :]) * (jax.nn.gelu(proj(2)) * mixed)

    o = jnp.dot(y.astype(jnp.bfloat16), wout_ref[...], preferred_element_type=jnp.float32)
    o_ref[...] = _ln_rows(ALPHA * x + o, g1_ref[...], b1_ref[...])


def _token_mix(x, w_in, b_gate, conv_w, conv_b, rg_w_a, rg_b_a, rg_w_x, rg_b_x, rg_lambda,
               gm_ln_g, gm_ln_b, gm_ws, gm_bs, w_out, ln1_g, ln1_b, tm=MIX_TM):
    bsz, s, d = x.shape
    nj = s // tm
    row = lambda v: v.reshape(1, d)
    wax = jnp.concatenate([rg_w_a, rg_w_x], axis=-1).astype(jnp.bfloat16)
    bs_full = jnp.repeat(gm_bs.T, d // GMLP_HEADS, axis=1)
    const2 = lambda shape: pl.BlockSpec(shape, lambda b, j: (0, 0))
    const3 = lambda shape: pl.BlockSpec(shape, lambda b, j: (0, 0, 0))
    return pl.pallas_call(
        _mix_kernel,
        grid=(bsz, nj),
        in_specs=[
            pl.BlockSpec((tm, d), lambda b, j: (b * nj + j, 0)),
            const2((d, w_in.shape[1])),
            const2((N_BRANCH, d)),
            const2((CONV_WIDTH, d)),
            const2((1, d)),
            const3(wax.shape),
            const2((1, d)), const2((1, d)), const2((1, d)),
            const2((1, d)), const2((1, d)),
            const3(gm_ws.shape),
            const2((CHUNK, d)),
            const2((d, d)),
            const2((1, d)), const2((1, d)),
        ],
        out_specs=pl.BlockSpec((tm, d), lambda b, j: (b * nj + j, 0)),
        out_shape=jax.ShapeDtypeStruct((bsz * s, d), jnp.float32),
        scratch_shapes=[pltpu.VMEM((tm + SUBLANES, d), jnp.float32),
                        pltpu.VMEM((1, d), jnp.float32)],
        compiler_params=pltpu.CompilerParams(dimension_semantics=("arbitrary", "arbitrary"),
                                             vmem_limit_bytes=MIX_VMEM_BYTES),
        name="token_mix",
    )(x.reshape(bsz * s, d), w_in.astype(jnp.bfloat16), b_gate, conv_w, row(conv_b), wax,
      row(rg_b_a), row(rg_b_x), row(rg_lambda), row(gm_ln_g), row(gm_ln_b), gm_ws, bs_full,
      w_out.astype(jnp.bfloat16), row(ln1_g), row(ln1_b))


def _sub_max(x):
    return jnp.max(x, axis=0, keepdims=True)


def _sub_min(x):
    return jnp.min(x, axis=0, keepdims=True)


def _top_of_rows(s, sub):
    depth = N_KEYS // SUBLANES
    lv = [s[j * SUBLANES:(j + 1) * SUBLANES] for j in range(depth)]
    ix = [sub + j * SUBLANES for j in range(depth)]
    for phase in range(depth):
        for j in range(phase % 2, depth - 1, 2):
            swap = lv[j + 1] > lv[j]
            lv[j], lv[j + 1] = jnp.where(swap, lv[j + 1], lv[j]), jnp.where(swap, lv[j], lv[j + 1])
            ix[j], ix[j + 1] = jnp.where(swap, ix[j + 1], ix[j]), jnp.where(swap, ix[j], ix[j + 1])
    vals, picked = [], []
    for t in range(TOPK):
        m = _sub_max(lv[0])
        p = _sub_min(jnp.where(lv[0] == m, ix[0], N_KEYS))
        hit = ix[0] == p
        vals.append(m)
        picked.append(p)
        for j in range(depth - 1 - t):
            lv[j] = jnp.where(hit, lv[j + 1], lv[j])
            ix[j] = jnp.where(hit, ix[j + 1], ix[j])
    return vals, picked


def _top_of_sums(v1, i1, v2, i2, sub):
    v1a = jnp.concatenate(v1[:SUBLANES], axis=0)
    v1b = jnp.concatenate(v1[SUBLANES:], axis=0)
    e1a = jnp.concatenate(i1[:SUBLANES], axis=0) * N_KEYS
    e1b = jnp.concatenate(i1[SUBLANES:], axis=0) * N_KEYS
    lv = [jnp.where(sub < TOPK // (b + 1), v1a + v2[b], NEG_INF) for b in range(TOPK)]
    ev = [e1a + i2[b] for b in range(TOPK)]
    lb = v1b + v2[0]
    eb = e1b + i2[0]
    col_a, col_b = sub, sub + SUBLANES
    vals, picked = [], []
    for t in range(TOPK):
        m = jnp.maximum(_sub_max(lv[0]), _sub_max(lb))
        p = jnp.minimum(_sub_min(jnp.where(lv[0] == m, col_a, TOPK)), _sub_min(jnp.where(lb == m, col_b, TOPK)))
        hit_a = col_a == p
        hit_b = col_b == p
        vals.append(m)
        picked.append(jnp.maximum(_sub_max(jnp.where(hit_a, ev[0], -1)), _sub_max(jnp.where(hit_b, eb, -1))))
        for j in range(TOPK - 1 - t):
            lv[j] = jnp.where(hit_a, lv[j + 1], lv[j])
            ev[j] = jnp.where(hit_a, ev[j + 1], ev[j])
        lb = jnp.where(hit_b, NEG_INF, lb)
    return vals, picked


def _route_kernel(x_ref, wq_ref, keys_ref, idx_ref, gate_ref, s_ref, it_ref, gt_ref):
    tm = x_ref.shape[0]
    q = jnp.dot(x_ref[...].astype(jnp.bfloat16), wq_ref[...], preferred_element_type=jnp.float32)
    for j in range(2 * PEER_HEADS):
        s_ref[j] = lax.dot_general(keys_ref[j], q[:, j * D_HALF:(j + 1) * D_HALF].astype(jnp.bfloat16),
                                   (((1,), (1,)), ((), ())), preferred_element_type=jnp.float32)

    sub = lax.broadcasted_iota(jnp.int32, (SUBLANES, ROUTE_W), 0)

    def tile(tl, carry):
        c0 = pl.multiple_of(tl * ROUTE_W, ROUTE_W)

        def head(h, c):
            v1, i1 = _top_of_rows(s_ref[2 * h, :, pl.ds(c0, ROUTE_W)], sub)
            v2, i2 = _top_of_rows(s_ref[2 * h + 1, :, pl.ds(c0, ROUTE_W)], sub)
            tv, te = _top_of_sums(v1, i1, v2, i2, sub)
            tv = jnp.concatenate(tv, axis=0)
            ex = jnp.exp(tv - tv[0:1])
            g = ex / jnp.sum(ex, axis=0, keepdims=True)
            r0 = pl.multiple_of(h * TOPK, TOPK)
            gt_ref[pl.ds(r0, TOPK), :] = g
            it_ref[pl.ds(r0, TOPK), :] = jnp.concatenate(te, axis=0)
            return c

        lax.fori_loop(0, PEER_HEADS, head, 0)
        gate_ref[pl.ds(c0, ROUTE_W), :] = gt_ref[...].T
        idx_ref[pl.ds(c0, ROUTE_W), :] = it_ref[...].T
        return carry

    lax.fori_loop(0, tm // ROUTE_W, tile, 0)


def _peer_route(x1, wq_bf16, keys_bf16, tm=ROUTE_TM):
    t, d = x1.shape
    nq = wq_bf16.shape[1]
    return pl.pallas_call(
        _route_kernel,
        grid=(t // tm,),
        in_specs=[pl.BlockSpec((tm, d), lambda i: (i, 0)),
                  pl.BlockSpec((d, nq), lambda i: (0, 0)),
                  pl.BlockSpec((2 * PEER_HEADS, N_KEYS, D_HALF), lambda i: (0, 0, 0))],
        out_specs=[pl.BlockSpec((tm, PEER_ROWS), lambda i: (i, 0)),
                   pl.BlockSpec((tm, PEER_ROWS), lambda i: (i, 0))],
        out_shape=[jax.ShapeDtypeStruct((t, PEER_ROWS), jnp.int32),
                   jax.ShapeDtypeStruct((t, PEER_ROWS), jnp.float32)],
        scratch_shapes=[pltpu.VMEM((2 * PEER_HEADS, N_KEYS, tm), jnp.float32),
                        pltpu.VMEM((PEER_ROWS, ROUTE_W), jnp.int32),
                        pltpu.VMEM((PEER_ROWS, ROUTE_W), jnp.float32)],
        compiler_params=pltpu.CompilerParams(dimension_semantics=("arbitrary",)),
        name="peer_route",
    )(x1, wq_bf16, keys_bf16)


def _sublane_sums(tiles, sub):
    step = SUBLANES // 2
    while step >= 1:
        low = (sub & step) == 0
        nxt = []
        for j in range(len(tiles) // 2):
            a, b = tiles[j], tiles[j + len(tiles) // 2]
            nxt.append(jnp.where(low, a, pltpu.roll(b, step, axis=0))
                       + jnp.where(low, pltpu.roll(a, SUBLANES - step, axis=0), b))
        tiles = nxt
        step //= 2
    return tiles[0]


def _peer_gather_kernel(idx_ref, x_ref, gate_ref, g2_ref, b2_ref, uv_hbm, o_ref, buf, sem):
    i = pl.program_id(0)
    nb = pl.num_programs(0) - 1
    slot = lax.rem(i, 2)
    prev = 1 - slot

    def start_rows(g, ks):
        for k in ks:
            e = idx_ref[0, 0, g * PEER_ROWS + k]
            pltpu.make_async_copy(uv_hbm.at[e], buf.at[slot, g, k],
                                  sem.at[slot, g // PEER_HALF]).start(priority=k % 2)

    groups = PEER_ROWS // SUBLANES

    def tile_of_row(g):
        return jnp.concatenate([x_ref[g:g + 1, c * LANES:(c + 1) * LANES] for c in range(D_TILES)], axis=0)

    def dots(g, x8, sub, fetch):
        parts = []
        for q in range(groups):
            prods = [buf[prev, g, q * SUBLANES + j, 0] * x8 for j in range(SUBLANES)]
            fetch(g, 2 * q)
            parts.append(_sublane_sums(prods, sub))
        return jnp.concatenate(parts, axis=0)

    def combine(g, x8, act, fetch):
        out = None
        for k in range(PEER_ROWS):
            term = buf[prev, g, k, 1] * act[k:k + 1, :]
            out = term if out is None else out + term
            if k % SUBLANES == SUBLANES - 1:
                fetch(g, 2 * (k // SUBLANES) + 1)
        p = ALPHA * x8 + out
        mu = jnp.mean(p, keepdims=True)
        pc = p - mu
        var = jnp.mean(pc * pc, keepdims=True)
        y8 = pc * lax.rsqrt(var + LN_EPS) * g2_ref[...] + b2_ref[...]
        for c in range(D_TILES):
            o_ref[g:g + 1, c * LANES:(c + 1) * LANES] = y8[c:c + 1, :]

    def finish_block(fetch):
        def arrived(half):
            rows = buf.at[prev, pl.ds(half * PEER_HALF, PEER_HALF)]
            pltpu.make_async_copy(rows, rows, sem.at[prev, half]).wait()

        gate_t = gate_ref[...].T
        lane = lax.broadcasted_iota(jnp.int32, gate_t.shape, 1)
        sub = lax.broadcasted_iota(jnp.int32, (SUBLANES, LANES), 0)
        arrived(0)
        x8 = tile_of_row(0)
        acc = dots(0, x8, sub, fetch)
        for g in range(PEER_G):
            h = jnp.sum(acc, axis=1, keepdims=True)
            gcol = jnp.sum(jnp.where(lane == g, gate_t, 0.0), axis=1, keepdims=True)
            act = jax.nn.gelu(h) * gcol
            x8_cur = x8
            if g + 1 < PEER_G:
                if (g + 1) % PEER_HALF == 0:
                    arrived((g + 1) // PEER_HALF)
                x8 = tile_of_row(g + 1)
                acc = dots(g + 1, x8, sub, fetch)
            combine(g, x8_cur, act, fetch)

    part = SUBLANES // 2

    @pl.when(i == 0)
    def _():
        def first(g, c):
            start_rows(g, range(PEER_ROWS))
            return c
        lax.fori_loop(0, PEER_G, first, 0)

    @pl.when(jnp.logical_and(i > 0, i < nb))
    def _():
        finish_block(lambda g, j: start_rows(g, range(j * part, (j + 1) * part)))

    @pl.when(i == nb)
    def _():
        finish_block(lambda g, j: None)


def _peer_gather(x1, idx, gate, u_tab, v_tab, ln_g, ln_b):
    t, d = x1.shape
    nb = t // PEER_G
    n_exp = u_tab.shape[0]
    uv = jnp.stack([u_tab, v_tab], axis=1).reshape(n_exp, 2, D_TILES, LANES)
    idx3 = idx.reshape(nb, 1, PEER_G * PEER_ROWS)
    cur = lambda i: (jnp.minimum(i, nb - 1), 0, 0)
    prv2 = lambda i: (jnp.maximum(i - 1, 0), 0)
    return pl.pallas_call(
        _peer_gather_kernel,
        grid=(nb + 1,),
        in_specs=[
            pl.BlockSpec((1, 1, PEER_G * PEER_ROWS), cur, memory_space=pltpu.SMEM),
            pl.BlockSpec((PEER_G, d), prv2),
            pl.BlockSpec((PEER_G, PEER_ROWS), prv2),
            pl.BlockSpec((D_TILES, LANES), lambda i: (0, 0)),
            pl.BlockSpec((D_TILES, LANES), lambda i: (0, 0)),
            pl.BlockSpec(memory_space=pl.ANY),
        ],
        out_specs=pl.BlockSpec((PEER_G, d), prv2),
        out_shape=jax.ShapeDtypeStruct((t, d), jnp.float32),
        scratch_shapes=[
            pltpu.VMEM((2, PEER_G, PEER_ROWS, 2, D_TILES, LANES), jnp.float32),
            pltpu.SemaphoreType.DMA((2, PEER_G // PEER_HALF)),
        ],
        compiler_params=pltpu.CompilerParams(dimension_semantics=("arbitrary",),
                                             vmem_limit_bytes=PEER_VMEM_BYTES),
        name="peer_gather",
    )(idx3, x1, gate, ln_g.reshape(D_TILES, LANES), ln_b.reshape(D_TILES, LANES), uv)


def kernel(x, w_in, b_gate, conv_w, conv_b, rg_w_a, rg_b_a, rg_w_x, rg_b_x, rg_lambda, gm_ln_g, gm_ln_b, gm_ws, gm_bs, w_out, ln1_g, ln1_b, peer_wq, peer_sub_keys, peer_u, peer_v, ln2_g, ln2_b):
    bsz, s, d = x.shape
    assert w_in.shape[0] == 1, "single-layer stack"
    l = 0
    x1 = _token_mix(x, w_in[l], b_gate[l], conv_w[l], conv_b[l], rg_w_a[l], rg_b_a[l], rg_w_x[l], rg_b_x[l],
                    rg_lambda[l], gm_ln_g[l], gm_ln_b[l], gm_ws[l], gm_bs[l], w_out[l], ln1_g[l], ln1_b[l])
    keys = peer_sub_keys[l].reshape(2 * PEER_HEADS, N_KEYS, D_HALF).astype(jnp.bfloat16)
    idx, gate = _peer_route(x1, peer_wq[l].astype(jnp.bfloat16), keys)
    x2 = _peer_gather(x1, idx, gate, peer_u[l], peer_v[l], ln2_g[l], ln2_b[l])
    return x2.reshape(bsz, s, d)
```

```python
import jax
import jax.numpy as jnp
from jax import lax
from jax.experimental import pallas as pl
from jax.experimental.pallas import tpu as pltpu

D_MODEL = 1024
RNN_HEADS = 8
CONV_WIDTH = 4
LRU_C = 8.0
GMLP_HEADS = 8
CHUNK = 128
N_BRANCH = 2
PEER_HEADS = 8
N_KEYS = 128
D_HALF = 128
TOPK = 16
ALPHA = 2.0 ** 0.25
LN_EPS = 1e-5
NEG_INF = float("-inf")

LANES = 128
SUBLANES = 8
D_TILES = D_MODEL // LANES
PEER_ROWS = PEER_HEADS * TOPK

MIX_TM = 512
ROUTE_TM = 512
ROUTE_W = 256
PEER_G = 16
PEER_LAG = 2
PEER_SLOTS = PEER_LAG + 1
MIX_VMEM_BYTES = 56 * 1024 * 1024
PEER_VMEM_BYTES = 57 * 1024 * 1024


def _ln_rows(y, g, b):
    mu = jnp.mean(y, axis=-1, keepdims=True)
    yc = y - mu
    var = jnp.mean(yc * yc, axis=-1, keepdims=True)
    return yc * lax.rsqrt(var + LN_EPS) * g + b


def _mix_kernel(x_ref, win_ref, bgate_ref, convw_ref, convb_ref, wax_ref, ba_ref, bx_ref, lam_ref,
                lng_ref, lnb_ref, ws_ref, bsf_ref, wout_ref, g1_ref, b1_ref,
                o_ref, xa_buf, h_ref):
    tm = x_ref.shape[0]
    d = D_MODEL
    hd = d // RNN_HEADS
    j = pl.program_id(1)

    @pl.when(j == 0)
    def _():
        xa_buf[0:SUBLANES, :] = jnp.zeros((SUBLANES, d), jnp.float32)
        h_ref[...] = jnp.zeros_like(h_ref)

    x = x_ref[...]
    xb = x.astype(jnp.bfloat16)

    def proj(k):
        return jnp.dot(xb, win_ref[:, k * d:(k + 1) * d], preferred_element_type=jnp.float32)

    xa_buf[SUBLANES:SUBLANES + tm, :] = proj(0)
    xc = convb_ref[...] + convw_ref[CONV_WIDTH - 1:CONV_WIDTH, :] * xa_buf[SUBLANES:SUBLANES + tm, :]
    for k in range(1, CONV_WIDTH):
        xc = xc + convw_ref[CONV_WIDTH - 1 - k:CONV_WIDTH - k, :] * xa_buf[SUBLANES - k:SUBLANES - k + tm, :]
    xa_buf[0:SUBLANES, :] = xa_buf[tm:tm + SUBLANES, :]

    xcb = xc.astype(jnp.bfloat16)
    pre = [jnp.dot(xcb[:, hh * hd:(hh + 1) * hd], wax_ref[hh], preferred_element_type=jnp.float32)
           for hh in range(RNN_HEADS)]
    r = jax.nn.sigmoid(jnp.concatenate([p[:, :hd] for p in pre], axis=1) + ba_ref[...])
    ig = jax.nn.sigmoid(jnp.concatenate([p[:, hd:] for p in pre], axis=1) + bx_ref[...])
    z = -lam_ref[...]
    softplus = jnp.maximum(z, 0.0) + jnp.log1p(jnp.exp(-jnp.abs(z)))
    log_a = (-LRU_C) * r * softplus
    a = jnp.exp(log_a)
    b = jnp.sqrt(1.0 - a * a) * (ig * xc)

    sub = lax.broadcasted_iota(jnp.int32, (SUBLANES, d), 0)
    carry = h_ref[...]
    hs = []
    for r in range(tm // SUBLANES):
        ag = a[r * SUBLANES:(r + 1) * SUBLANES]
        bg = b[r * SUBLANES:(r + 1) * SUBLANES]
        sh = 1
        while sh < SUBLANES:
            a_prev = jnp.where(sub >= sh, pltpu.roll(ag, sh, axis=0), 1.0)
            b_prev = jnp.where(sub >= sh, pltpu.roll(bg, sh, axis=0), 0.0)
            bg = ag * b_prev + bg
            ag = ag * a_prev
            sh *= 2
        hg = bg + ag * carry
        carry = hg[SUBLANES - 1:SUBLANES, :]
        hs.append(hg)
    h = jnp.concatenate(hs, axis=0)
    h_ref[...] = carry
    y = jax.nn.sigmoid(proj(4) + bgate_ref[0:1, :]) * (jax.nn.gelu(proj(1)) * h)

    vn = _ln_rows(jax.nn.gelu(proj(3)), lng_ref[...], lnb_ref[...]).astype(jnp.bfloat16)
    tri = (lax.broadcasted_iota(jnp.int32, (CHUNK, CHUNK), 0)
           >= lax.broadcasted_iota(jnp.int32, (CHUNK, CHUNK), 1))
    chunks = []
    for c in range(tm // CHUNK):
        heads = []
        for hh in range(GMLP_HEADS):
            w = jnp.where(tri, ws_ref[hh], 0.0).astype(jnp.bfloat16)
            heads.append(jnp.dot(w, vn[c * CHUNK:(c + 1) * CHUNK, hh * hd:(hh + 1) * hd],
                                 preferred_element_type=jnp.float32))
        chunks.append(jnp.concatenate(heads, axis=1) + bsf_ref[...])
    mixed = jnp.concatenate(chunks, axis=0)
    y = y + jax.nn.sigmoid(proj(5) + bgate_ref[1:2, :]) * (jax.nn.gelu(proj(2)) * mixed)

    o = jnp.dot(y.astype(jnp.bfloat16), wout_ref[...], preferred_element_type=jnp.float32)
    o_ref[...] = _ln_rows(ALPHA * x + o, g1_ref[...], b1_ref[...])


def _token_mix(x, w_in, b_gate, conv_w, conv_b, rg_w_a, rg_b_a, rg_w_x, rg_b_x, rg_lambda,
               gm_ln_g, gm_ln_b, gm_ws, gm_bs, w_out, ln1_g, ln1_b, tm=MIX_TM):
    bsz, s, d = x.shape
    nj = s // tm
    row = lambda v: v.reshape(1, d)
    wax = jnp.concatenate([rg_w_a, rg_w_x], axis=-1).astype(jnp.bfloat16)
    bs_full = jnp.repeat(gm_bs.T, d // GMLP_HEADS, axis=1)
    const2 = lambda shape: pl.BlockSpec(shape, lambda b, j: (0, 0))
    const3 = lambda shape: pl.BlockSpec(shape, lambda b, j: (0, 0, 0))
    return pl.pallas_call(
        _mix_kernel,
        grid=(bsz, nj),
        in_specs=[
            pl.BlockSpec((tm, d), lambda b, j: (b * nj + j, 0)),
            const2((d, w_in.shape[1])),
            const2((N_BRANCH, d)),
            const2((CONV_WIDTH, d)),
            const2((1, d)),
            const3(wax.shape),
            const2((1, d)), const2((1, d)), const2((1, d)),
            const2((1, d)), const2((1, d)),
            const3(gm_ws.shape),
            const2((CHUNK, d)),
            const2((d, d)),
            const2((1, d)), const2((1, d)),
        ],
        out_specs=pl.BlockSpec((tm, d), lambda b, j: (b * nj + j, 0)),
        out_shape=jax.ShapeDtypeStruct((bsz * s, d), jnp.float32),
        scratch_shapes=[pltpu.VMEM((tm + SUBLANES, d), jnp.float32),
                        pltpu.VMEM((1, d), jnp.float32)],
        compiler_params=pltpu.CompilerParams(dimension_semantics=("arbitrary", "arbitrary"),
                                             vmem_limit_bytes=MIX_VMEM_BYTES),
        name="token_mix",
    )(x.reshape(bsz * s, d), w_in.astype(jnp.bfloat16), b_gate, conv_w, row(conv_b), wax,
      row(rg_b_a), row(rg_b_x), row(rg_lambda), row(gm_ln_g), row(gm_ln_b), gm_ws, bs_full,
      w_out.astype(jnp.bfloat16), row(ln1_g), row(ln1_b))


def _sub_max(x):
    return jnp.max(x, axis=0, keepdims=True)


def _sub_min(x):
    return jnp.min(x, axis=0, keepdims=True)


def _top_of_rows(s, sub):
    depth = N_KEYS // SUBLANES
    lv = [s[j * SUBLANES:(j + 1) * SUBLANES] for j in range(depth)]
    ix = [sub + j * SUBLANES for j in range(depth)]
    for phase in range(depth):
        for j in range(phase % 2, depth - 1, 2):
            swap = lv[j + 1] > lv[j]
            lv[j], lv[j + 1] = jnp.where(swap, lv[j + 1], lv[j]), jnp.where(swap, lv[j], lv[j + 1])
            ix[j], ix[j + 1] = jnp.where(swap, ix[j + 1], ix[j]), jnp.where(swap, ix[j], ix[j + 1])
    vals, picked = [], []
    for t in range(TOPK):
        m = _sub_max(lv[0])
        p = _sub_min(jnp.where(lv[0] == m, ix[0], N_KEYS))
        hit = ix[0] == p
        vals.append(m)
        picked.append(p)
        for j in range(depth - 1 - t):
            lv[j] = jnp.where(hit, lv[j + 1], lv[j])
            ix[j] = jnp.where(hit, ix[j + 1], ix[j])
    return vals, picked


def _top_of_sums(v1, i1, v2, i2, sub):
    v1a = jnp.concatenate(v1[:SUBLANES], axis=0)
    v1b = jnp.concatenate(v1[SUBLANES:], axis=0)
    e1a = jnp.concatenate(i1[:SUBLANES], axis=0) * N_KEYS
    e1b = jnp.concatenate(i1[SUBLANES:], axis=0) * N_KEYS
    lv = [jnp.where(sub < TOPK // (b + 1), v1a + v2[b], NEG_INF) for b in range(TOPK)]
    ev = [e1a + i2[b] for b in range(TOPK)]
    lb = v1b + v2[0]
    eb = e1b + i2[0]
    col_a, col_b = sub, sub + SUBLANES
    vals, picked = [], []
    for t in range(TOPK):
        m = jnp.maximum(_sub_max(lv[0]), _sub_max(lb))
        p = jnp.minimum(_sub_min(jnp.where(lv[0] == m, col_a, TOPK)), _sub_min(jnp.where(lb == m, col_b, TOPK)))
        hit_a = col_a == p
        hit_b = col_b == p
        vals.append(m)
        picked.append(jnp.maximum(_sub_max(jnp.where(hit_a, ev[0], -1)), _sub_max(jnp.where(hit_b, eb, -1))))
        for j in range(TOPK - 1 - t):
            lv[j] = jnp.where(hit_a, lv[j + 1], lv[j])
            ev[j] = jnp.where(hit_a, ev[j + 1], ev[j])
        lb = jnp.where(hit_b, NEG_INF, lb)
    return vals, picked


def _route_kernel(x_ref, wq_ref, keys_ref, idx_ref, gate_ref, s_ref, it_ref, gt_ref):
    tm = x_ref.shape[0]
    q = jnp.dot(x_ref[...].astype(jnp.bfloat16), wq_ref[...], preferred_element_type=jnp.float32)
    for j in range(2 * PEER_HEADS):
        s_ref[j] = lax.dot_general(keys_ref[j], q[:, j * D_HALF:(j + 1) * D_HALF].astype(jnp.bfloat16),
                                   (((1,), (1,)), ((), ())), preferred_element_type=jnp.float32)

    sub = lax.broadcasted_iota(jnp.int32, (SUBLANES, ROUTE_W), 0)

    def tile(tl, carry):
        c0 = pl.multiple_of(tl * ROUTE_W, ROUTE_W)

        def head(h, c):
            v1, i1 = _top_of_rows(s_ref[2 * h, :, pl.ds(c0, ROUTE_W)], sub)
            v2, i2 = _top_of_rows(s_ref[2 * h + 1, :, pl.ds(c0, ROUTE_W)], sub)
            tv, te = _top_of_sums(v1, i1, v2, i2, sub)
            tv = jnp.concatenate(tv, axis=0)
            ex = jnp.exp(tv - tv[0:1])
            g = ex / jnp.sum(ex, axis=0, keepdims=True)
            r0 = pl.multiple_of(h * TOPK, TOPK)
            gt_ref[pl.ds(r0, TOPK), :] = g
            it_ref[pl.ds(r0, TOPK), :] = jnp.concatenate(te, axis=0)
            return c

        lax.fori_loop(0, PEER_HEADS, head, 0)
        gate_ref[pl.ds(c0, ROUTE_W), :] = gt_ref[...].T
        idx_ref[pl.ds(c0, ROUTE_W), :] = it_ref[...].T
        return carry

    lax.fori_loop(0, tm // ROUTE_W, tile, 0)


def _peer_route(x1, wq_bf16, keys_bf16, tm=ROUTE_TM):
    t, d = x1.shape
    nq = wq_bf16.shape[1]
    return pl.pallas_call(
        _route_kernel,
        grid=(t // tm,),
        in_specs=[pl.BlockSpec((tm, d), lambda i: (i, 0)),
                  pl.BlockSpec((d, nq), lambda i: (0, 0)),
                  pl.BlockSpec((2 * PEER_HEADS, N_KEYS, D_HALF), lambda i: (0, 0, 0))],
        out_specs=[pl.BlockSpec((tm, PEER_ROWS), lambda i: (i, 0)),
                   pl.BlockSpec((tm, PEER_ROWS), lambda i: (i, 0))],
        out_shape=[jax.ShapeDtypeStruct((t, PEER_ROWS), jnp.int32),
                   jax.ShapeDtypeStruct((t, PEER_ROWS), jnp.float32)],
        scratch_shapes=[pltpu.VMEM((2 * PEER_HEADS, N_KEYS, tm), jnp.float32),
                        pltpu.VMEM((PEER_ROWS, ROUTE_W), jnp.int32),
                        pltpu.VMEM((PEER_ROWS, ROUTE_W), jnp.float32)],
        compiler_params=pltpu.CompilerParams(dimension_semantics=("arbitrary",)),
        name="peer_route",
    )(x1, wq_bf16, keys_bf16)


def _sublane_sums(tiles, sub):
    step = SUBLANES // 2
    while step >= 1:
        low = (sub & step) == 0
        nxt = []
        for j in range(len(tiles) // 2):
            a, b = tiles[j], tiles[j + len(tiles) // 2]
            nxt.append(jnp.where(low, a, pltpu.roll(b, step, axis=0))
                       + jnp.where(low, pltpu.roll(a, SUBLANES - step, axis=0), b))
        tiles = nxt
        step //= 2
    return tiles[0]


def _peer_gather_kernel(idx_ref, x_ref, gate_ref, g2_ref, b2_ref, uv_hbm, o_ref, buf, sem):
    i = pl.program_id(0)
    nb = pl.num_programs(0) - PEER_LAG
    slot = lax.rem(i, PEER_SLOTS)
    prev = lax.rem(i + PEER_SLOTS - PEER_LAG, PEER_SLOTS)

    def start_rows(g, ks):
        for k in ks:
            e = idx_ref[0, 0, g * PEER_ROWS + k]
            pltpu.make_async_copy(uv_hbm.at[e], buf.at[slot, g, k],
                                  sem.at[slot]).start(priority=k % 2)

    groups = PEER_ROWS // SUBLANES

    def tile_of_row(g):
        return jnp.concatenate([x_ref[g:g + 1, c * LANES:(c + 1) * LANES] for c in range(D_TILES)], axis=0)

    def dots(g, x8, sub, fetch):
        parts = []
        for q in range(groups):
            prods = [buf[prev, g, q * SUBLANES + j, 0] * x8 for j in range(SUBLANES)]
            fetch(g, 2 * q)
            parts.append(_sublane_sums(prods, sub))
        return jnp.concatenate(parts, axis=0)

    def combine(g, x8, act, fetch):
        out = None
        for k in range(PEER_ROWS):
            term = buf[prev, g, k, 1] * act[k:k + 1, :]
            out = term if out is None else out + term
            if k % SUBLANES == SUBLANES - 1:
                fetch(g, 2 * (k // SUBLANES) + 1)
        p = ALPHA * x8 + out
        mu = jnp.mean(p, keepdims=True)
        pc = p - mu
        var = jnp.mean(pc * pc, keepdims=True)
        y8 = pc * lax.rsqrt(var + LN_EPS) * g2_ref[...] + b2_ref[...]
        for c in range(D_TILES):
            o_ref[g:g + 1, c * LANES:(c + 1) * LANES] = y8[c:c + 1, :]

    def finish_block(fetch):
        pltpu.make_async_copy(buf.at[prev], buf.at[prev], sem.at[prev]).wait()
        gate_t = gate_ref[...].T
        lane = lax.broadcasted_iota(jnp.int32, gate_t.shape, 1)
        sub = lax.broadcasted_iota(jnp.int32, (SUBLANES, LANES), 0)
        x8 = tile_of_row(0)
        acc = dots(0, x8, sub, fetch)
        for g in range(PEER_G):
            h = jnp.sum(acc, axis=1, keepdims=True)
            gcol = jnp.sum(jnp.where(lane == g, gate_t, 0.0), axis=1, keepdims=True)
            act = jax.nn.gelu(h) * gcol
            x8_cur = x8
            if g + 1 < PEER_G:
                x8 = tile_of_row(g + 1)
                acc = dots(g + 1, x8, sub, fetch)
            combine(g, x8_cur, act, fetch)

    part = SUBLANES // 2

    @pl.when(i < PEER_LAG)
    def _():
        def first(g, c):
            start_rows(g, range(PEER_ROWS))
            return c
        lax.fori_loop(0, PEER_G, first, 0)

    @pl.when(jnp.logical_and(i >= PEER_LAG, i < nb))
    def _():
        finish_block(lambda g, j: start_rows(g, range(j * part, (j + 1) * part)))

    @pl.when(i >= nb)
    def _():
        finish_block(lambda g, j: None)


def _peer_gather(x1, idx, gate, u_tab, v_tab, ln_g, ln_b):
    t, d = x1.shape
    nb = t // PEER_G
    n_exp = u_tab.shape[0]
    uv = jnp.stack([u_tab, v_tab], axis=1).reshape(n_exp, 2, D_TILES, LANES)
    idx3 = idx.reshape(nb, 1, PEER_G * PEER_ROWS)
    cur = lambda i: (jnp.minimum(i, nb - 1), 0, 0)
    prv2 = lambda i: (jnp.maximum(i - PEER_LAG, 0), 0)
    return pl.pallas_call(
        _peer_gather_kernel,
        grid=(nb + PEER_LAG,),
        in_specs=[
            pl.BlockSpec((1, 1, PEER_G * PEER_ROWS), cur, memory_space=pltpu.SMEM),
            pl.BlockSpec((PEER_G, d), prv2),
            pl.BlockSpec((PEER_G, PEER_ROWS), prv2),
            pl.BlockSpec((D_TILES, LANES), lambda i: (0, 0)),
            pl.BlockSpec((D_TILES, LANES), lambda i: (0, 0)),
            pl.BlockSpec(memory_space=pl.ANY),
        ],
        out_specs=pl.BlockSpec((PEER_G, d), prv2),
        out_shape=jax.ShapeDtypeStruct((t, d), jnp.float32),
        scratch_shapes=[
            pltpu.VMEM((PEER_SLOTS, PEER_G, PEER_ROWS, 2, D_TILES, LANES), jnp.float32),
            pltpu.SemaphoreType.DMA((PEER_SLOTS,)),
        ],
        compiler_params=pltpu.CompilerParams(dimension_semantics=("arbitrary",),
                                             vmem_limit_bytes=PEER_VMEM_BYTES),
        name="peer_gather",
    )(idx3, x1, gate, ln_g.reshape(D_TILES, LANES), ln_b.reshape(D_TILES, LANES), uv)


def kernel(x, w_in, b_gate, conv_w, conv_b, rg_w_a, rg_b_a, rg_w_x, rg_b_x, rg_lambda, gm_ln_g, gm_ln_b, gm_ws, gm_bs, w_out, ln1_g, ln1_b, peer_wq, peer_sub_keys, peer_u, peer_v, ln2_g, ln2_b):
    bsz, s, d = x.shape
    assert w_in.shape[0] == 1, "single-layer stack"
    l = 0
    x1 = _token_mix(x, w_in[l], b_gate[l], conv_w[l], conv_b[l], rg_w_a[l], rg_b_a[l], rg_w_x[l], rg_b_x[l],
                    rg_lambda[l], gm_ln_g[l], gm_ln_b[l], gm_ws[l], gm_bs[l], w_out[l], ln1_g[l], ln1_b[l])
    keys = peer_sub_keys[l].reshape(2 * PEER_HEADS, N_KEYS, D_HALF).astype(jnp.bfloat16)
    idx, gate = _peer_route(x1, peer_wq[l].astype(jnp.bfloat16), keys)
    x2 = _peer_gather(x1, idx, gate, peer_u[l], peer_v[l], ln2_g[l], ln2_b[l])
    return x2.reshape(bsz, s, d)
```

```python
import jax
import jax.numpy as jnp
from jax import lax
from jax.experimental import pallas as pl
from jax.experimental.pallas import tpu as pltpu

D_MODEL = 1024
RNN_HEADS = 8
CONV_WIDTH = 4
LRU_C = 8.0
GMLP_HEADS = 8
CHUNK = 128
N_BRANCH = 2
PEER_HEADS = 8
N_KEYS = 128
D_HALF = 128
TOPK = 16
ALPHA = 2.0 ** 0.25
LN_EPS = 1e-5
NEG_INF = float("-inf")

LANES = 128
SUBLANES = 8
D_TILES = D_MODEL // LANES
PEER_ROWS = PEER_HEADS * TOPK

MIX_TM = 512
ROUTE_TM = 512
ROUTE_W = 256
UV_PACK_ROWS = 256
PEER_G = 16
PEER_LAG = 2
PEER_SLOTS = PEER_LAG + 1
MIX_VMEM_BYTES = 56 * 1024 * 1024
PEER_VMEM_BYTES = 57 * 1024 * 1024


def _ln_rows(y, g, b):
    mu = jnp.mean(y, axis=-1, keepdims=True)
    yc = y - mu
    var = jnp.mean(yc * yc, axis=-1, keepdims=True)
    return yc * lax.rsqrt(var + LN_EPS) * g + b


def _mix_kernel(x_ref, win_ref, bgate_ref, convw_ref, convb_ref, wax_ref, ba_ref, bx_ref, lam_ref,
                lng_ref, lnb_ref, ws_ref, bsf_ref, wout_ref, g1_ref, b1_ref,
                o_ref, xa_buf, h_ref):
    tm = x_ref.shape[0]
    d = D_MODEL
    hd = d // RNN_HEADS
    j = pl.program_id(1)

    @pl.when(j == 0)
    def _():
        xa_buf[0:SUBLANES, :] = jnp.zeros((SUBLANES, d), jnp.float32)
        h_ref[...] = jnp.zeros_like(h_ref)

    x = x_ref[...]
    xb = x.astype(jnp.bfloat16)

    def proj(k):
        return jnp.dot(xb, win_ref[:, k * d:(k + 1) * d], preferred_element_type=jnp.float32)

    xa_buf[SUBLANES:SUBLANES + tm, :] = proj(0)
    xc = convb_ref[...] + convw_ref[CONV_WIDTH - 1:CONV_WIDTH, :] * xa_buf[SUBLANES:SUBLANES + tm, :]
    for k in range(1, CONV_WIDTH):
        xc = xc + convw_ref[CONV_WIDTH - 1 - k:CONV_WIDTH - k, :] * xa_buf[SUBLANES - k:SUBLANES - k + tm, :]
    xa_buf[0:SUBLANES, :] = xa_buf[tm:tm + SUBLANES, :]

    xcb = xc.astype(jnp.bfloat16)
    pre = [jnp.dot(xcb[:, hh * hd:(hh + 1) * hd], wax_ref[hh], preferred_element_type=jnp.float32)
           for hh in range(RNN_HEADS)]
    r = jax.nn.sigmoid(jnp.concatenate([p[:, :hd] for p in pre], axis=1) + ba_ref[...])
    ig = jax.nn.sigmoid(jnp.concatenate([p[:, hd:] for p in pre], axis=1) + bx_ref[...])
    z = -lam_ref[...]
    softplus = jnp.maximum(z, 0.0) + jnp.log1p(jnp.exp(-jnp.abs(z)))
    log_a = (-LRU_C) * r * softplus
    a = jnp.exp(log_a)
    b = jnp.sqrt(1.0 - a * a) * (ig * xc)

    sub = lax.broadcasted_iota(jnp.int32, (SUBLANES, d), 0)
    carry = h_ref[...]
    hs = []
    for r in range(tm // SUBLANES):
        ag = a[r * SUBLANES:(r + 1) * SUBLANES]
        bg = b[r * SUBLANES:(r + 1) * SUBLANES]
        sh = 1
        while sh < SUBLANES:
            a_prev = jnp.where(sub >= sh, pltpu.roll(ag, sh, axis=0), 1.0)
            b_prev = jnp.where(sub >= sh, pltpu.roll(bg, sh, axis=0), 0.0)
            bg = ag * b_prev + bg
            ag = ag * a_prev
            sh *= 2
        hg = bg + ag * carry
        carry = hg[SUBLANES - 1:SUBLANES, :]
        hs.append(hg)
    h = jnp.concatenate(hs, axis=0)
    h_ref[...] = carry
    y = jax.nn.sigmoid(proj(4) + bgate_ref[0:1, :]) * (jax.nn.gelu(proj(1)) * h)

    vn = _ln_rows(jax.nn.gelu(proj(3)), lng_ref[...], lnb_ref[...]).astype(jnp.bfloat16)
    tri = (lax.broadcasted_iota(jnp.int32, (CHUNK, CHUNK), 0)
           >= lax.broadcasted_iota(jnp.int32, (CHUNK, CHUNK), 1))
    chunks = []
    for c in range(tm // CHUNK):
        heads = []
        for hh in range(GMLP_HEADS):
            w = jnp.where(tri, ws_ref[hh], 0.0).astype(jnp.bfloat16)
            heads.append(jnp.dot(w, vn[c * CHUNK:(c + 1) * CHUNK, hh * hd:(hh + 1) * hd],
                                 preferred_element_type=jnp.float32))
        chunks.append(jnp.concatenate(heads, axis=1) + bsf_ref[...])
    mixed = jnp.concatenate(chunks, axis=0)
    y = y + jax.nn.sigmoid(proj(5) + bgate_ref[1:2, :]) * (jax.nn.gelu(proj(2)) * mixed)

    o = jnp.dot(y.astype(jnp.bfloat16), wout_ref[...], preferred_element_type=jnp.float32)
    o_ref[...] = _ln_rows(ALPHA * x + o, g1_ref[...], b1_ref[...])


def _token_mix(x, w_in, b_gate, conv_w, conv_b, rg_w_a, rg_b_a, rg_w_x, rg_b_x, rg_lambda,
               gm_ln_g, gm_ln_b, gm_ws, gm_bs, w_out, ln1_g, ln1_b, tm=MIX_TM):
    bsz, s, d = x.shape
    nj = s // tm
    row = lambda v: v.reshape(1, d)
    wax = jnp.concatenate([rg_w_a, rg_w_x], axis=-1).astype(jnp.bfloat16)
    bs_full = jnp.repeat(gm_bs.T, d // GMLP_HEADS, axis=1)
    const2 = lambda shape: pl.BlockSpec(shape, lambda b, j: (0, 0))
    const3 = lambda shape: pl.BlockSpec(shape, lambda b, j: (0, 0, 0))
    return pl.pallas_call(
        _mix_kernel,
        grid=(bsz, nj),
        in_specs=[
            pl.BlockSpec((tm, d), lambda b, j: (b * nj + j, 0)),
            const2((d, w_in.shape[1])),
            const2((N_BRANCH, d)),
            const2((CONV_WIDTH, d)),
            const2((1, d)),
            const3(wax.shape),
            const2((1, d)), const2((1, d)), const2((1, d)),
            const2((1, d)), const2((1, d)),
            const3(gm_ws.shape),
            const2((CHUNK, d)),
            const2((d, d)),
            const2((1, d)), const2((1, d)),
        ],
        out_specs=pl.BlockSpec((tm, d), lambda b, j: (b * nj + j, 0)),
        out_shape=jax.ShapeDtypeStruct((bsz * s, d), jnp.float32),
        scratch_shapes=[pltpu.VMEM((tm + SUBLANES, d), jnp.float32),
                        pltpu.VMEM((1, d), jnp.float32)],
        compiler_params=pltpu.CompilerParams(dimension_semantics=("arbitrary", "arbitrary"),
                                             vmem_limit_bytes=MIX_VMEM_BYTES),
        name="token_mix",
    )(x.reshape(bsz * s, d), w_in.astype(jnp.bfloat16), b_gate, conv_w, row(conv_b), wax,
      row(rg_b_a), row(rg_b_x), row(rg_lambda), row(gm_ln_g), row(gm_ln_b), gm_ws, bs_full,
      w_out.astype(jnp.bfloat16), row(ln1_g), row(ln1_b))


def _sub_max(x):
    return jnp.max(x, axis=0, keepdims=True)


def _sub_min(x):
    return jnp.min(x, axis=0, keepdims=True)


def _top_of_rows(s, sub):
    depth = N_KEYS // SUBLANES
    lv = [s[j * SUBLANES:(j + 1) * SUBLANES] for j in range(depth)]
    ix = [sub + j * SUBLANES for j in range(depth)]
    for phase in range(depth):
        for j in range(phase % 2, depth - 1, 2):
            swap = lv[j + 1] > lv[j]
            lv[j], lv[j + 1] = jnp.where(swap, lv[j + 1], lv[j]), jnp.where(swap, lv[j], lv[j + 1])
            ix[j], ix[j + 1] = jnp.where(swap, ix[j + 1], ix[j]), jnp.where(swap, ix[j], ix[j + 1])
    vals, picked = [], []
    for t in range(TOPK):
        m = _sub_max(lv[0])
        p = _sub_min(jnp.where(lv[0] == m, ix[0], N_KEYS))
        hit = ix[0] == p
        vals.append(m)
        picked.append(p)
        for j in range(depth - 1 - t):
            lv[j] = jnp.where(hit, lv[j + 1], lv[j])
            ix[j] = jnp.where(hit, ix[j + 1], ix[j])
    return vals, picked


def _top_of_sums(v1, i1, v2, i2, sub):
    v1a = jnp.concatenate(v1[:SUBLANES], axis=0)
    v1b = jnp.concatenate(v1[SUBLANES:], axis=0)
    e1a = jnp.concatenate(i1[:SUBLANES], axis=0) * N_KEYS
    e1b = jnp.concatenate(i1[SUBLANES:], axis=0) * N_KEYS
    lv = [jnp.where(sub < TOPK // (b + 1), v1a + v2[b], NEG_INF) for b in range(TOPK)]
    ev = [e1a + i2[b] for b in range(TOPK)]
    lb = v1b + v2[0]
    eb = e1b + i2[0]
    col_a, col_b = sub, sub + SUBLANES
    vals, picked = [], []
    for t in range(TOPK):
        m = jnp.maximum(_sub_max(lv[0]), _sub_max(lb))
        p = jnp.minimum(_sub_min(jnp.where(lv[0] == m, col_a, TOPK)), _sub_min(jnp.where(lb == m, col_b, TOPK)))
        hit_a = col_a == p
        hit_b = col_b == p
        vals.append(m)
        picked.append(jnp.maximum(_sub_max(jnp.where(hit_a, ev[0], -1)), _sub_max(jnp.where(hit_b, eb, -1))))
        for j in range(TOPK - 1 - t):
            lv[j] = jnp.where(hit_a, lv[j + 1], lv[j])
            ev[j] = jnp.where(hit_a, ev[j + 1], ev[j])
        lb = jnp.where(hit_b, NEG_INF, lb)
    return vals, picked


def _route_kernel(x_ref, wq_ref, keys_ref, idx_ref, gate_ref, s_ref, it_ref, gt_ref):
    tm = x_ref.shape[0]
    q = jnp.dot(x_ref[...].astype(jnp.bfloat16), wq_ref[...], preferred_element_type=jnp.float32)
    for j in range(2 * PEER_HEADS):
        s_ref[j] = lax.dot_general(keys_ref[j], q[:, j * D_HALF:(j + 1) * D_HALF].astype(jnp.bfloat16),
                                   (((1,), (1,)), ((), ())), preferred_element_type=jnp.float32)

    sub = lax.broadcasted_iota(jnp.int32, (SUBLANES, ROUTE_W), 0)

    def tile(tl, carry):
        c0 = pl.multiple_of(tl * ROUTE_W, ROUTE_W)

        def head(h, c):
            v1, i1 = _top_of_rows(s_ref[2 * h, :, pl.ds(c0, ROUTE_W)], sub)
            v2, i2 = _top_of_rows(s_ref[2 * h + 1, :, pl.ds(c0, ROUTE_W)], sub)
            tv, te = _top_of_sums(v1, i1, v2, i2, sub)
            tv = jnp.concatenate(tv, axis=0)
            ex = jnp.exp(tv - tv[0:1])
            g = ex / jnp.sum(ex, axis=0, keepdims=True)
            r0 = pl.multiple_of(h * TOPK, TOPK)
            gt_ref[pl.ds(r0, TOPK), :] = g
            it_ref[pl.ds(r0, TOPK), :] = jnp.concatenate(te, axis=0)
            return c

        lax.fori_loop(0, PEER_HEADS, head, 0)
        gate_ref[pl.ds(c0, ROUTE_W), :] = gt_ref[...].T
        idx_ref[pl.ds(c0, ROUTE_W), :] = it_ref[...].T
        return carry

    lax.fori_loop(0, tm // ROUTE_W, tile, 0)


def _peer_route(x1, wq_bf16, keys_bf16, tm=ROUTE_TM):
    t, d = x1.shape
    nq = wq_bf16.shape[1]
    return pl.pallas_call(
        _route_kernel,
        grid=(t // tm,),
        in_specs=[pl.BlockSpec((tm, d), lambda i: (i, 0)),
                  pl.BlockSpec((d, nq), lambda i: (0, 0)),
                  pl.BlockSpec((2 * PEER_HEADS, N_KEYS, D_HALF), lambda i: (0, 0, 0))],
        out_specs=[pl.BlockSpec((tm, PEER_ROWS), lambda i: (i, 0)),
                   pl.BlockSpec((tm, PEER_ROWS), lambda i: (i, 0))],
        out_shape=[jax.ShapeDtypeStruct((t, PEER_ROWS), jnp.int32),
                   jax.ShapeDtypeStruct((t, PEER_ROWS), jnp.float32)],
        scratch_shapes=[pltpu.VMEM((2 * PEER_HEADS, N_KEYS, tm), jnp.float32),
                        pltpu.VMEM((PEER_ROWS, ROUTE_W), jnp.int32),
                        pltpu.VMEM((PEER_ROWS, ROUTE_W), jnp.float32)],
        compiler_params=pltpu.CompilerParams(dimension_semantics=("arbitrary",)),
        name="peer_route",
    )(x1, wq_bf16, keys_bf16)


def _sublane_sums(tiles, sub):
    step = SUBLANES // 2
    while step >= 1:
        low = (sub & step) == 0
        nxt = []
        for j in range(len(tiles) // 2):
            a, b = tiles[j], tiles[j + len(tiles) // 2]
            nxt.append(jnp.where(low, a, pltpu.roll(b, step, axis=0))
                       + jnp.where(low, pltpu.roll(a, SUBLANES - step, axis=0), b))
        tiles = nxt
        step //= 2
    return tiles[0]


def _peer_gather_kernel(idx_ref, x_ref, gate_ref, g2_ref, b2_ref, uv_hbm, o_ref, buf, sem):
    i = pl.program_id(0)
    nb = pl.num_programs(0) - PEER_LAG
    slot = lax.rem(i, PEER_SLOTS)
    prev = lax.rem(i + PEER_SLOTS - PEER_LAG, PEER_SLOTS)

    def start_rows(g, ks):
        for k in ks:
            e = idx_ref[0, 0, g * PEER_ROWS + k]
            pltpu.make_async_copy(uv_hbm.at[e], buf.at[slot, g, k],
                                  sem.at[slot]).start(priority=k % 2)

    groups = PEER_ROWS // SUBLANES

    def tile_of_row(g):
        return jnp.concatenate([x_ref[g:g + 1, c * LANES:(c + 1) * LANES] for c in range(D_TILES)], axis=0)

    def dots(g, x8, sub, fetch):
        parts = []
        for q in range(groups):
            prods = [buf[prev, g, q * SUBLANES + j, 0] * x8 for j in range(SUBLANES)]
            fetch(g, 2 * q)
            parts.append(_sublane_sums(prods, sub))
        return jnp.concatenate(parts, axis=0)

    def combine(g, x8, act, fetch):
        out = None
        for k in range(PEER_ROWS):
            term = buf[prev, g, k, 1] * act[k:k + 1, :]
            out = term if out is None else out + term
            if k % SUBLANES == SUBLANES - 1:
                fetch(g, 2 * (k // SUBLANES) + 1)
        p = ALPHA * x8 + out
        mu = jnp.mean(p, keepdims=True)
        pc = p - mu
        var = jnp.mean(pc * pc, keepdims=True)
        y8 = pc * lax.rsqrt(var + LN_EPS) * g2_ref[...] + b2_ref[...]
        for c in range(D_TILES):
            o_ref[g:g + 1, c * LANES:(c + 1) * LANES] = y8[c:c + 1, :]

    def finish_block(fetch):
        pltpu.make_async_copy(buf.at[prev], buf.at[prev], sem.at[prev]).wait()
        gate_t = gate_ref[...].T
        lane = lax.broadcasted_iota(jnp.int32, gate_t.shape, 1)
        sub = lax.broadcasted_iota(jnp.int32, (SUBLANES, LANES), 0)
        x8 = tile_of_row(0)
        acc = dots(0, x8, sub, fetch)
        for g in range(PEER_G):
            h = jnp.sum(acc, axis=1, keepdims=True)
            gcol = jnp.sum(jnp.where(lane == g, gate_t, 0.0), axis=1, keepdims=True)
            act = jax.nn.gelu(h) * gcol
            x8_cur = x8
            if g + 1 < PEER_G:
                x8 = tile_of_row(g + 1)
                acc = dots(g + 1, x8, sub, fetch)
            combine(g, x8_cur, act, fetch)

    part = SUBLANES // 2

    @pl.when(i < PEER_LAG)
    def _():
        def first(g, c):
            start_rows(g, range(PEER_ROWS))
            return c
        lax.fori_loop(0, PEER_G, first, 0)

    @pl.when(jnp.logical_and(i >= PEER_LAG, i < nb))
    def _():
        finish_block(lambda g, j: start_rows(g, range(j * part, (j + 1) * part)))

    @pl.when(i >= nb)
    def _():
        finish_block(lambda g, j: None)


def _uv_pack_kernel(u_ref, v_ref, o_ref):
    rows = u_ref.shape[0]
    o_ref[:, 0] = u_ref[...].reshape(rows, D_TILES, LANES)
    o_ref[:, 1] = v_ref[...].reshape(rows, D_TILES, LANES)


def _uv_pack(u_tab, v_tab, rows=UV_PACK_ROWS):
    n_exp, d = u_tab.shape
    return pl.pallas_call(
        _uv_pack_kernel,
        grid=(n_exp // rows,),
        in_specs=[pl.BlockSpec((rows, d), lambda i: (i, 0)), pl.BlockSpec((rows, d), lambda i: (i, 0))],
        out_specs=pl.BlockSpec((rows, 2, D_TILES, LANES), lambda i: (i, 0, 0, 0)),
        out_shape=jax.ShapeDtypeStruct((n_exp, 2, D_TILES, LANES), jnp.float32),
        name="uv_pack",
    )(u_tab, v_tab)


def _peer_gather(x1, idx, gate, u_tab, v_tab, ln_g, ln_b):
    t, d = x1.shape
    nb = t // PEER_G
    uv = _uv_pack(u_tab, v_tab)
    idx3 = idx.reshape(nb, 1, PEER_G * PEER_ROWS)
    cur = lambda i: (jnp.minimum(i, nb - 1), 0, 0)
    prv2 = lambda i: (jnp.maximum(i - PEER_LAG, 0), 0)
    return pl.pallas_call(
        _peer_gather_kernel,
        grid=(nb + PEER_LAG,),
        in_specs=[
            pl.BlockSpec((1, 1, PEER_G * PEER_ROWS), cur, memory_space=pltpu.SMEM),
            pl.BlockSpec((PEER_G, d), prv2),
            pl.BlockSpec((PEER_G, PEER_ROWS), prv2),
            pl.BlockSpec((D_TILES, LANES), lambda i: (0, 0)),
            pl.BlockSpec((D_TILES, LANES), lambda i: (0, 0)),
            pl.BlockSpec(memory_space=pl.ANY),
        ],
        out_specs=pl.BlockSpec((PEER_G, d), prv2),
        out_shape=jax.ShapeDtypeStruct((t, d), jnp.float32),
        scratch_shapes=[
            pltpu.VMEM((PEER_SLOTS, PEER_G, PEER_ROWS, 2, D_TILES, LANES), jnp.float32),
            pltpu.SemaphoreType.DMA((PEER_SLOTS,)),
        ],
        compiler_params=pltpu.CompilerParams(dimension_semantics=("arbitrary",),
                                             vmem_limit_bytes=PEER_VMEM_BYTES),
        name="peer_gather",
    )(idx3, x1, gate, ln_g.reshape(D_TILES, LANES), ln_b.reshape(D_TILES, LANES), uv)


def kernel(x, w_in, b_gate, conv_w, conv_b, rg_w_a, rg_b_a, rg_w_x, rg_b_x, rg_lambda, gm_ln_g, gm_ln_b, gm_ws, gm_bs, w_out, ln1_g, ln1_b, peer_wq, peer_sub_keys, peer_u, peer_v, ln2_g, ln2_b):
    bsz, s, d = x.shape
    assert w_in.shape[0] == 1, "single-layer stack"
    l = 0
    x1 = _token_mix(x, w_in[l], b_gate[l], conv_w[l], conv_b[l], rg_w_a[l], rg_b_a[l], rg_w_x[l], rg_b_x[l],
                    rg_lambda[l], gm_ln_g[l], gm_ln_b[l], gm_ws[l], gm_bs[l], w_out[l], ln1_g[l], ln1_b[l])
    keys = peer_sub_keys[l].reshape(2 * PEER_HEADS, N_KEYS, D_HALF).astype(jnp.bfloat16)
    idx, gate = _peer_route(x1, peer_wq[l].astype(jnp.bfloat16), keys)
    x2 = _peer_gather(x1, idx, gate, peer_u[l], peer_v[l], ln2_g[l], ln2_b[l])
    return x2.reshape(bsz, s, d)
```

```python
import jax
import jax.numpy as jnp
from jax import lax
from jax.experimental import pallas as pl
from jax.experimental.pallas import tpu as pltpu

D_MODEL = 1024
RNN_HEADS = 8
CONV_WIDTH = 4
LRU_C = 8.0
GMLP_HEADS = 8
CHUNK = 128
N_BRANCH = 2
PEER_HEADS = 8
N_KEYS = 128
D_HALF = 128
TOPK = 16
ALPHA = 2.0 ** 0.25
LN_EPS = 1e-5
NEG_INF = float("-inf")

LANES = 128
SUBLANES = 8
D_TILES = D_MODEL // LANES
PEER_ROWS = PEER_HEADS * TOPK

MIX_TM = 512
ROUTE_TM = 512
ROUTE_W = 256
UV_PACK_ROWS = 1024
PEER_G = 16
PEER_LAG = 2
PEER_SLOTS = PEER_LAG + 1
MIB = 1024 * 1024
MIX_VMEM_BYTES = 56 * MIB
PEER_BUF_BYTES = PEER_SLOTS * PEER_G * PEER_ROWS * 2 * D_MODEL * 4
PEER_VMEM_BYTES = PEER_BUF_BYTES + 9 * MIB


def _ln_rows(y, g, b):
    mu = jnp.mean(y, axis=-1, keepdims=True)
    yc = y - mu
    var = jnp.mean(yc * yc, axis=-1, keepdims=True)
    return yc * lax.rsqrt(var + LN_EPS) * g + b


def _mix_kernel(x_ref, win_ref, bgate_ref, convw_ref, convb_ref, wax_ref, ba_ref, bx_ref, lam_ref,
                lng_ref, lnb_ref, ws_ref, bsf_ref, wout_ref, g1_ref, b1_ref,
                o_ref, xa_buf, h_ref):
    tm = x_ref.shape[0]
    d = D_MODEL
    hd = d // RNN_HEADS
    j = pl.program_id(1)

    @pl.when(j == 0)
    def _():
        xa_buf[0:SUBLANES, :] = jnp.zeros((SUBLANES, d), jnp.float32)
        h_ref[...] = jnp.zeros_like(h_ref)

    x = x_ref[...]
    xb = x.astype(jnp.bfloat16)

    def proj(k):
        return jnp.dot(xb, win_ref[:, k * d:(k + 1) * d], preferred_element_type=jnp.float32)

    xa_buf[SUBLANES:SUBLANES + tm, :] = proj(0)
    xc = convb_ref[...] + convw_ref[CONV_WIDTH - 1:CONV_WIDTH, :] * xa_buf[SUBLANES:SUBLANES + tm, :]
    for k in range(1, CONV_WIDTH):
        xc = xc + convw_ref[CONV_WIDTH - 1 - k:CONV_WIDTH - k, :] * xa_buf[SUBLANES - k:SUBLANES - k + tm, :]
    xa_buf[0:SUBLANES, :] = xa_buf[tm:tm + SUBLANES, :]

    xcb = xc.astype(jnp.bfloat16)
    pre = [jnp.dot(xcb[:, hh * hd:(hh + 1) * hd], wax_ref[hh], preferred_element_type=jnp.float32)
           for hh in range(RNN_HEADS)]
    r = jax.nn.sigmoid(jnp.concatenate([p[:, :hd] for p in pre], axis=1) + ba_ref[...])
    ig = jax.nn.sigmoid(jnp.concatenate([p[:, hd:] for p in pre], axis=1) + bx_ref[...])
    z = -lam_ref[...]
    softplus = jnp.maximum(z, 0.0) + jnp.log1p(jnp.exp(-jnp.abs(z)))
    log_a = (-LRU_C) * r * softplus
    a = jnp.exp(log_a)
    b = jnp.sqrt(1.0 - a * a) * (ig * xc)

    sub = lax.broadcasted_iota(jnp.int32, (SUBLANES, d), 0)
    carry = h_ref[...]
    hs = []
    for r in range(tm // SUBLANES):
        ag = a[r * SUBLANES:(r + 1) * SUBLANES]
        bg = b[r * SUBLANES:(r + 1) * SUBLANES]
        sh = 1
        while sh < SUBLANES:
            a_prev = jnp.where(sub >= sh, pltpu.roll(ag, sh, axis=0), 1.0)
            b_prev = jnp.where(sub >= sh, pltpu.roll(bg, sh, axis=0), 0.0)
            bg = ag * b_prev + bg
            ag = ag * a_prev
            sh *= 2
        hg = bg + ag * carry
        carry = hg[SUBLANES - 1:SUBLANES, :]
        hs.append(hg)
    h = jnp.concatenate(hs, axis=0)
    h_ref[...] = carry
    y = jax.nn.sigmoid(proj(4) + bgate_ref[0:1, :]) * (jax.nn.gelu(proj(1)) * h)

    vn = _ln_rows(jax.nn.gelu(proj(3)), lng_ref[...], lnb_ref[...]).astype(jnp.bfloat16)
    tri = (lax.broadcasted_iota(jnp.int32, (CHUNK, CHUNK), 0)
           >= lax.broadcasted_iota(jnp.int32, (CHUNK, CHUNK), 1))
    chunks = []
    for c in range(tm // CHUNK):
        heads = []
        for hh in range(GMLP_HEADS):
            w = jnp.where(tri, ws_ref[hh], 0.0).astype(jnp.bfloat16)
            heads.append(jnp.dot(w, vn[c * CHUNK:(c + 1) * CHUNK, hh * hd:(hh + 1) * hd],
                                 preferred_element_type=jnp.float32))
        chunks.append(jnp.concatenate(heads, axis=1) + bsf_ref[...])
    mixed = jnp.concatenate(chunks, axis=0)
    y = y + jax.nn.sigmoid(proj(5) + bgate_ref[1:2, :]) * (jax.nn.gelu(proj(2)) * mixed)

    o = jnp.dot(y.astype(jnp.bfloat16), wout_ref[...], preferred_element_type=jnp.float32)
    o_ref[...] = _ln_rows(ALPHA * x + o, g1_ref[...], b1_ref[...])


def _token_mix(x, w_in, b_gate, conv_w, conv_b, rg_w_a, rg_b_a, rg_w_x, rg_b_x, rg_lambda,
               gm_ln_g, gm_ln_b, gm_ws, gm_bs, w_out, ln1_g, ln1_b, tm=MIX_TM):
    bsz, s, d = x.shape
    nj = s // tm
    row = lambda v: v.reshape(1, d)
    wax = jnp.concatenate([rg_w_a, rg_w_x], axis=-1).astype(jnp.bfloat16)
    bs_full = jnp.repeat(gm_bs.T, d // GMLP_HEADS, axis=1)
    const2 = lambda shape: pl.BlockSpec(shape, lambda b, j: (0, 0))
    const3 = lambda shape: pl.BlockSpec(shape, lambda b, j: (0, 0, 0))
    return pl.pallas_call(
        _mix_kernel,
        grid=(bsz, nj),
        in_specs=[
            pl.BlockSpec((tm, d), lambda b, j: (b * nj + j, 0)),
            const2((d, w_in.shape[1])),
            const2((N_BRANCH, d)),
            const2((CONV_WIDTH, d)),
            const2((1, d)),
            const3(wax.shape),
            const2((1, d)), const2((1, d)), const2((1, d)),
            const2((1, d)), const2((1, d)),
            const3(gm_ws.shape),
            const2((CHUNK, d)),
            const2((d, d)),
            const2((1, d)), const2((1, d)),
        ],
        out_specs=pl.BlockSpec((tm, d), lambda b, j: (b * nj + j, 0)),
        out_shape=jax.ShapeDtypeStruct((bsz * s, d), jnp.float32),
        scratch_shapes=[pltpu.VMEM((tm + SUBLANES, d), jnp.float32),
                        pltpu.VMEM((1, d), jnp.float32)],
        compiler_params=pltpu.CompilerParams(dimension_semantics=("arbitrary", "arbitrary"),
                                             vmem_limit_bytes=MIX_VMEM_BYTES),
        name="token_mix",
    )(x.reshape(bsz * s, d), w_in.astype(jnp.bfloat16), b_gate, conv_w, row(conv_b), wax,
      row(rg_b_a), row(rg_b_x), row(rg_lambda), row(gm_ln_g), row(gm_ln_b), gm_ws, bs_full,
      w_out.astype(jnp.bfloat16), row(ln1_g), row(ln1_b))


def _sub_max(x):
    return jnp.max(x, axis=0, keepdims=True)


def _sub_min(x):
    return jnp.min(x, axis=0, keepdims=True)


def _top_of_rows(s, sub):
    depth = N_KEYS // SUBLANES
    lv = [s[j * SUBLANES:(j + 1) * SUBLANES] for j in range(depth)]
    ix = [sub + j * SUBLANES for j in range(depth)]
    for phase in range(depth):
        for j in range(phase % 2, depth - 1, 2):
            swap = lv[j + 1] > lv[j]
            lv[j], lv[j + 1] = jnp.where(swap, lv[j + 1], lv[j]), jnp.where(swap, lv[j], lv[j + 1])
            ix[j], ix[j + 1] = jnp.where(swap, ix[j + 1], ix[j]), jnp.where(swap, ix[j], ix[j + 1])
    vals, picked = [], []
    for t in range(TOPK):
        m = _sub_max(lv[0])
        p = _sub_min(jnp.where(lv[0] == m, ix[0], N_KEYS))
        hit = ix[0] == p
        vals.append(m)
        picked.append(p)
        for j in range(depth - 1 - t):
            lv[j] = jnp.where(hit, lv[j + 1], lv[j])
            ix[j] = jnp.where(hit, ix[j + 1], ix[j])
    return vals, picked


def _top_of_sums(v1, i1, v2, i2, sub):
    v1a = jnp.concatenate(v1[:SUBLANES], axis=0)
    v1b = jnp.concatenate(v1[SUBLANES:], axis=0)
    e1a = jnp.concatenate(i1[:SUBLANES], axis=0) * N_KEYS
    e1b = jnp.concatenate(i1[SUBLANES:], axis=0) * N_KEYS
    lv = [jnp.where(sub < TOPK // (b + 1), v1a + v2[b], NEG_INF) for b in range(TOPK)]
    ev = [e1a + i2[b] for b in range(TOPK)]
    lb = v1b + v2[0]
    eb = e1b + i2[0]
    col_a, col_b = sub, sub + SUBLANES
    vals, picked = [], []
    for t in range(TOPK):
        m = jnp.maximum(_sub_max(lv[0]), _sub_max(lb))
        p = jnp.minimum(_sub_min(jnp.where(lv[0] == m, col_a, TOPK)), _sub_min(jnp.where(lb == m, col_b, TOPK)))
        hit_a = col_a == p
        hit_b = col_b == p
        vals.append(m)
        picked.append(jnp.maximum(_sub_max(jnp.where(hit_a, ev[0], -1)), _sub_max(jnp.where(hit_b, eb, -1))))
        for j in range(TOPK - 1 - t):
            lv[j] = jnp.where(hit_a, lv[j + 1], lv[j])
            ev[j] = jnp.where(hit_a, ev[j + 1], ev[j])
        lb = jnp.where(hit_b, NEG_INF, lb)
    return vals, picked


def _route_kernel(x_ref, wq_ref, keys_ref, idx_ref, gate_ref, s_ref, it_ref, gt_ref):
    tm = x_ref.shape[0]
    q = jnp.dot(x_ref[...].astype(jnp.bfloat16), wq_ref[...], preferred_element_type=jnp.float32)
    for j in range(2 * PEER_HEADS):
        s_ref[j] = lax.dot_general(keys_ref[j], q[:, j * D_HALF:(j + 1) * D_HALF].astype(jnp.bfloat16),
                                   (((1,), (1,)), ((), ())), preferred_element_type=jnp.float32)

    sub = lax.broadcasted_iota(jnp.int32, (SUBLANES, ROUTE_W), 0)

    def tile(tl, carry):
        c0 = pl.multiple_of(tl * ROUTE_W, ROUTE_W)

        def head(h, c):
            v1, i1 = _top_of_rows(s_ref[2 * h, :, pl.ds(c0, ROUTE_W)], sub)
            v2, i2 = _top_of_rows(s_ref[2 * h + 1, :, pl.ds(c0, ROUTE_W)], sub)
            tv, te = _top_of_sums(v1, i1, v2, i2, sub)
            tv = jnp.concatenate(tv, axis=0)
            ex = jnp.exp(tv - tv[0:1])
            g = ex / jnp.sum(ex, axis=0, keepdims=True)
            r0 = pl.multiple_of(h * TOPK, TOPK)
            gt_ref[pl.ds(r0, TOPK), :] = g
            it_ref[pl.ds(r0, TOPK), :] = jnp.concatenate(te, axis=0)
            return c

        lax.fori_loop(0, PEER_HEADS, head, 0)
        gate_ref[pl.ds(c0, ROUTE_W), :] = gt_ref[...].T
        idx_ref[pl.ds(c0, ROUTE_W), :] = it_ref[...].T
        return carry

    lax.fori_loop(0, tm // ROUTE_W, tile, 0)


def _peer_route(x1, wq_bf16, keys_bf16, tm=ROUTE_TM):
    t, d = x1.shape
    nq = wq_bf16.shape[1]
    return pl.pallas_call(
        _route_kernel,
        grid=(t // tm,),
        in_specs=[pl.BlockSpec((tm, d), lambda i: (i, 0)),
                  pl.BlockSpec((d, nq), lambda i: (0, 0)),
                  pl.BlockSpec((2 * PEER_HEADS, N_KEYS, D_HALF), lambda i: (0, 0, 0))],
        out_specs=[pl.BlockSpec((tm, PEER_ROWS), lambda i: (i, 0)),
                   pl.BlockSpec((tm, PEER_ROWS), lambda i: (i, 0))],
        out_shape=[jax.ShapeDtypeStruct((t, PEER_ROWS), jnp.int32),
                   jax.ShapeDtypeStruct((t, PEER_ROWS), jnp.float32)],
        scratch_shapes=[pltpu.VMEM((2 * PEER_HEADS, N_KEYS, tm), jnp.float32),
                        pltpu.VMEM((PEER_ROWS, ROUTE_W), jnp.int32),
                        pltpu.VMEM((PEER_ROWS, ROUTE_W), jnp.float32)],
        compiler_params=pltpu.CompilerParams(dimension_semantics=("arbitrary",)),
        name="peer_route",
    )(x1, wq_bf16, keys_bf16)


def _sublane_sums(tiles, sub):
    step = SUBLANES // 2
    while step >= 1:
        low = (sub & step) == 0
        nxt = []
        for j in range(len(tiles) // 2):
            a, b = tiles[j], tiles[j + len(tiles) // 2]
            nxt.append(jnp.where(low, a, pltpu.roll(b, step, axis=0))
                       + jnp.where(low, pltpu.roll(a, SUBLANES - step, axis=0), b))
        tiles = nxt
        step //= 2
    return tiles[0]


def _peer_gather_kernel(idx_ref, x_ref, gate_ref, g2_ref, b2_ref, uv_hbm, o_ref, buf, sem):
    i = pl.program_id(0)
    nb = pl.num_programs(0) - PEER_LAG
    slot = lax.rem(i, PEER_SLOTS)
    prev = lax.rem(i + PEER_SLOTS - PEER_LAG, PEER_SLOTS)

    def start_rows(g, ks):
        for k in ks:
            e = idx_ref[0, 0, g * PEER_ROWS + k]
            pltpu.make_async_copy(uv_hbm.at[e], buf.at[slot, g, k],
                                  sem.at[slot]).start(priority=k % 2)

    groups = PEER_ROWS // SUBLANES

    def tile_of_row(g):
        return jnp.concatenate([x_ref[g:g + 1, c * LANES:(c + 1) * LANES] for c in range(D_TILES)], axis=0)

    def dots(g, x8, sub, fetch):
        parts = []
        for q in range(groups):
            prods = [buf[prev, g, q * SUBLANES + j, 0] * x8 for j in range(SUBLANES)]
            fetch(g, 2 * q)
            parts.append(_sublane_sums(prods, sub))
        return jnp.concatenate(parts, axis=0)

    def combine(g, x8, act, fetch):
        out = None
        for k in range(PEER_ROWS):
            term = buf[prev, g, k, 1] * act[k:k + 1, :]
            out = term if out is None else out + term
            if k % SUBLANES == SUBLANES - 1:
                fetch(g, 2 * (k // SUBLANES) + 1)
        p = ALPHA * x8 + out
        mu = jnp.mean(p, keepdims=True)
        pc = p - mu
        var = jnp.mean(pc * pc, keepdims=True)
        y8 = pc * lax.rsqrt(var + LN_EPS) * g2_ref[...] + b2_ref[...]
        for c in range(D_TILES):
            o_ref[g:g + 1, c * LANES:(c + 1) * LANES] = y8[c:c + 1, :]

    def finish_block(fetch):
        pltpu.make_async_copy(buf.at[prev], buf.at[prev], sem.at[prev]).wait()
        gate_t = gate_ref[...].T
        lane = lax.broadcasted_iota(jnp.int32, gate_t.shape, 1)
        sub = lax.broadcasted_iota(jnp.int32, (SUBLANES, LANES), 0)
        x8 = tile_of_row(0)
        acc = dots(0, x8, sub, fetch)
        for g in range(PEER_G):
            h = jnp.sum(acc, axis=1, keepdims=True)
            gcol = jnp.sum(jnp.where(lane == g, gate_t, 0.0), axis=1, keepdims=True)
            act = jax.nn.gelu(h) * gcol
            x8_cur = x8
            if g + 1 < PEER_G:
                x8 = tile_of_row(g + 1)
                acc = dots(g + 1, x8, sub, fetch)
            combine(g, x8_cur, act, fetch)

    part = SUBLANES // 2

    @pl.when(i < PEER_LAG)
    def _():
        def first(g, c):
            start_rows(g, range(PEER_ROWS))
            return c
        lax.fori_loop(0, PEER_G, first, 0)

    @pl.when(jnp.logical_and(i >= PEER_LAG, i < nb))
    def _():
        finish_block(lambda g, j: start_rows(g, range(j * part, (j + 1) * part)))

    @pl.when(i >= nb)
    def _():
        finish_block(lambda g, j: None)


def _uv_pack_kernel(u_ref, v_ref, o_ref):
    rows = u_ref.shape[0]
    o_ref[:, 0] = u_ref[...].reshape(rows, D_TILES, LANES)
    o_ref[:, 1] = v_ref[...].reshape(rows, D_TILES, LANES)


def _uv_pack(u_tab, v_tab, rows=UV_PACK_ROWS):
    n_exp, d = u_tab.shape
    return pl.pallas_call(
        _uv_pack_kernel,
        grid=(n_exp // rows,),
        in_specs=[pl.BlockSpec((rows, d), lambda i: (i, 0)), pl.BlockSpec((rows, d), lambda i: (i, 0))],
        out_specs=pl.BlockSpec((rows, 2, D_TILES, LANES), lambda i: (i, 0, 0, 0)),
        out_shape=jax.ShapeDtypeStruct((n_exp, 2, D_TILES, LANES), jnp.float32),
        name="uv_pack",
    )(u_tab, v_tab)


def _peer_gather(x1, idx, gate, u_tab, v_tab, ln_g, ln_b):
    t, d = x1.shape
    nb = t // PEER_G
    uv = _uv_pack(u_tab, v_tab)
    idx3 = idx.reshape(nb, 1, PEER_G * PEER_ROWS)
    cur = lambda i: (jnp.minimum(i, nb - 1), 0, 0)
    prv2 = lambda i: (jnp.maximum(i - PEER_LAG, 0), 0)
    return pl.pallas_call(
        _peer_gather_kernel,
        grid=(nb + PEER_LAG,),
        in_specs=[
            pl.BlockSpec((1, 1, PEER_G * PEER_ROWS), cur, memory_space=pltpu.SMEM),
            pl.BlockSpec((PEER_G, d), prv2),
            pl.BlockSpec((PEER_G, PEER_ROWS), prv2),
            pl.BlockSpec((D_TILES, LANES), lambda i: (0, 0)),
            pl.BlockSpec((D_TILES, LANES), lambda i: (0, 0)),
            pl.BlockSpec(memory_space=pl.ANY),
        ],
        out_specs=pl.BlockSpec((PEER_G, d), prv2),
        out_shape=jax.ShapeDtypeStruct((t, d), jnp.float32),
        scratch_shapes=[
            pltpu.VMEM((PEER_SLOTS, PEER_G, PEER_ROWS, 2, D_TILES, LANES), jnp.float32),
            pltpu.SemaphoreType.DMA((PEER_SLOTS,)),
        ],
        compiler_params=pltpu.CompilerParams(dimension_semantics=("arbitrary",),
                                             vmem_limit_bytes=PEER_VMEM_BYTES),
        name="peer_gather",
    )(idx3, x1, gate, ln_g.reshape(D_TILES, LANES), ln_b.reshape(D_TILES, LANES), uv)


def kernel(x, w_in, b_gate, conv_w, conv_b, rg_w_a, rg_b_a, rg_w_x, rg_b_x, rg_lambda, gm_ln_g, gm_ln_b, gm_ws, gm_bs, w_out, ln1_g, ln1_b, peer_wq, peer_sub_keys, peer_u, peer_v, ln2_g, ln2_b):
    bsz, s, d = x.shape
    assert w_in.shape[0] == 1, "single-layer stack"
    l = 0
    x1 = _token_mix(x, w_in[l], b_gate[l], conv_w[l], conv_b[l], rg_w_a[l], rg_b_a[l], rg_w_x[l], rg_b_x[l],
                    rg_lambda[l], gm_ln_g[l], gm_ln_b[l], gm_ws[l], gm_bs[l], w_out[l], ln1_g[l], ln1_b[l])
    keys = peer_sub_keys[l].reshape(2 * PEER_HEADS, N_KEYS, D_HALF).astype(jnp.bfloat16)
    idx, gate = _peer_route(x1, peer_wq[l].astype(jnp.bfloat16), keys)
    x2 = _peer_gather(x1, idx, gate, peer_u[l], peer_v[l], ln2_g[l], ln2_b[l])
    return x2.reshape(bsz, s, d)
```

```python
import jax
import jax.numpy as jnp
from jax import lax
from jax.experimental import pallas as pl
from jax.experimental.pallas import tpu as pltpu

D_MODEL = 1024
RNN_HEADS = 8
CONV_WIDTH = 4
LRU_C = 8.0
GMLP_HEADS = 8
CHUNK = 128
N_BRANCH = 2
PEER_HEADS = 8
N_KEYS = 128
D_HALF = 128
TOPK = 16
ALPHA = 2.0 ** 0.25
LN_EPS = 1e-5
NEG_INF = float("-inf")

LANES = 128
SUBLANES = 8
D_TILES = D_MODEL // LANES
PEER_ROWS = PEER_HEADS * TOPK

MIX_TM = 512
ROUTE_TM = 512
ROUTE_W = 512
UV_PACK_ROWS = 1024
PEER_G = 16
PEER_LAG = 2
PEER_SLOTS = PEER_LAG + 1
MIB = 1024 * 1024
MIX_VMEM_BYTES = 56 * MIB
PEER_BUF_BYTES = PEER_SLOTS * PEER_G * PEER_ROWS * 2 * D_MODEL * 4
PEER_VMEM_BYTES = PEER_BUF_BYTES + 9 * MIB


def _ln_rows(y, g, b):
    mu = jnp.mean(y, axis=-1, keepdims=True)
    yc = y - mu
    var = jnp.mean(yc * yc, axis=-1, keepdims=True)
    return yc * lax.rsqrt(var + LN_EPS) * g + b


def _mix_kernel(x_ref, win_ref, bgate_ref, convw_ref, convb_ref, wax_ref, ba_ref, bx_ref, lam_ref,
                lng_ref, lnb_ref, ws_ref, bsf_ref, wout_ref, g1_ref, b1_ref,
                o_ref, xa_buf, h_ref):
    tm = x_ref.shape[0]
    d = D_MODEL
    hd = d // RNN_HEADS
    j = pl.program_id(1)

    @pl.when(j == 0)
    def _():
        xa_buf[0:SUBLANES, :] = jnp.zeros((SUBLANES, d), jnp.float32)
        h_ref[...] = jnp.zeros_like(h_ref)

    x = x_ref[...]
    xb = x.astype(jnp.bfloat16)

    def proj(k):
        return jnp.dot(xb, win_ref[:, k * d:(k + 1) * d], preferred_element_type=jnp.float32)

    xa_buf[SUBLANES:SUBLANES + tm, :] = proj(0)
    xc = convb_ref[...] + convw_ref[CONV_WIDTH - 1:CONV_WIDTH, :] * xa_buf[SUBLANES:SUBLANES + tm, :]
    for k in range(1, CONV_WIDTH):
        xc = xc + convw_ref[CONV_WIDTH - 1 - k:CONV_WIDTH - k, :] * xa_buf[SUBLANES - k:SUBLANES - k + tm, :]
    xa_buf[0:SUBLANES, :] = xa_buf[tm:tm + SUBLANES, :]

    xcb = xc.astype(jnp.bfloat16)
    pre = [jnp.dot(xcb[:, hh * hd:(hh + 1) * hd], wax_ref[hh], preferred_element_type=jnp.float32)
           for hh in range(RNN_HEADS)]
    r = jax.nn.sigmoid(jnp.concatenate([p[:, :hd] for p in pre], axis=1) + ba_ref[...])
    ig = jax.nn.sigmoid(jnp.concatenate([p[:, hd:] for p in pre], axis=1) + bx_ref[...])
    z = -lam_ref[...]
    softplus = jnp.maximum(z, 0.0) + jnp.log1p(jnp.exp(-jnp.abs(z)))
    log_a = (-LRU_C) * r * softplus
    a = jnp.exp(log_a)
    b = jnp.sqrt(1.0 - a * a) * (ig * xc)

    sub = lax.broadcasted_iota(jnp.int32, (SUBLANES, d), 0)
    carry = h_ref[...]
    hs = []
    for r in range(tm // SUBLANES):
        ag = a[r * SUBLANES:(r + 1) * SUBLANES]
        bg = b[r * SUBLANES:(r + 1) * SUBLANES]
        sh = 1
        while sh < SUBLANES:
            a_prev = jnp.where(sub >= sh, pltpu.roll(ag, sh, axis=0), 1.0)
            b_prev = jnp.where(sub >= sh, pltpu.roll(bg, sh, axis=0), 0.0)
            bg = ag * b_prev + bg
            ag = ag * a_prev
            sh *= 2
        hg = bg + ag * carry
        carry = hg[SUBLANES - 1:SUBLANES, :]
        hs.append(hg)
    h = jnp.concatenate(hs, axis=0)
    h_ref[...] = carry
    y = jax.nn.sigmoid(proj(4) + bgate_ref[0:1, :]) * (jax.nn.gelu(proj(1)) * h)

    vn = _ln_rows(jax.nn.gelu(proj(3)), lng_ref[...], lnb_ref[...]).astype(jnp.bfloat16)
    tri = (lax.broadcasted_iota(jnp.int32, (CHUNK, CHUNK), 0)
           >= lax.broadcasted_iota(jnp.int32, (CHUNK, CHUNK), 1))
    chunks = []
    for c in range(tm // CHUNK):
        heads = []
        for hh in range(GMLP_HEADS):
            w = jnp.where(tri, ws_ref[hh], 0.0).astype(jnp.bfloat16)
            heads.append(jnp.dot(w, vn[c * CHUNK:(c + 1) * CHUNK, hh * hd:(hh + 1) * hd],
                                 preferred_element_type=jnp.float32))
        chunks.append(jnp.concatenate(heads, axis=1) + bsf_ref[...])
    mixed = jnp.concatenate(chunks, axis=0)
    y = y + jax.nn.sigmoid(proj(5) + bgate_ref[1:2, :]) * (jax.nn.gelu(proj(2)) * mixed)

    o = jnp.dot(y.astype(jnp.bfloat16), wout_ref[...], preferred_element_type=jnp.float32)
    o_ref[...] = _ln_rows(ALPHA * x + o, g1_ref[...], b1_ref[...])


def _token_mix(x, w_in, b_gate, conv_w, conv_b, rg_w_a, rg_b_a, rg_w_x, rg_b_x, rg_lambda,
               gm_ln_g, gm_ln_b, gm_ws, gm_bs, w_out, ln1_g, ln1_b, tm=MIX_TM):
    bsz, s, d = x.shape
    nj = s // tm
    row = lambda v: v.reshape(1, d)
    wax = jnp.concatenate([rg_w_a, rg_w_x], axis=-1).astype(jnp.bfloat16)
    bs_full = jnp.repeat(gm_bs.T, d // GMLP_HEADS, axis=1)
    const2 = lambda shape: pl.BlockSpec(shape, lambda b, j: (0, 0))
    const3 = lambda shape: pl.BlockSpec(shape, lambda b, j: (0, 0, 0))
    return pl.pallas_call(
        _mix_kernel,
        grid=(bsz, nj),
        in_specs=[
            pl.BlockSpec((tm, d), lambda b, j: (b * nj + j, 0)),
            const2((d, w_in.shape[1])),
            const2((N_BRANCH, d)),
            const2((CONV_WIDTH, d)),
            const2((1, d)),
            const3(wax.shape),
            const2((1, d)), const2((1, d)), const2((1, d)),
            const2((1, d)), const2((1, d)),
            const3(gm_ws.shape),
            const2((CHUNK, d)),
            const2((d, d)),
            const2((1, d)), const2((1, d)),
        ],
        out_specs=pl.BlockSpec((tm, d), lambda b, j: (b * nj + j, 0)),
        out_shape=jax.ShapeDtypeStruct((bsz * s, d), jnp.float32),
        scratch_shapes=[pltpu.VMEM((tm + SUBLANES, d), jnp.float32),
                        pltpu.VMEM((1, d), jnp.float32)],
        compiler_params=pltpu.CompilerParams(dimension_semantics=("arbitrary", "arbitrary"),
                                             vmem_limit_bytes=MIX_VMEM_BYTES),
        name="token_mix",
    )(x.reshape(bsz * s, d), w_in.astype(jnp.bfloat16), b_gate, conv_w, row(conv_b), wax,
      row(rg_b_a), row(rg_b_x), row(rg_lambda), row(gm_ln_g), row(gm_ln_b), gm_ws, bs_full,
      w_out.astype(jnp.bfloat16), row(ln1_g), row(ln1_b))


def _sub_max(x):
    return jnp.max(x, axis=0, keepdims=True)


def _sub_min(x):
    return jnp.min(x, axis=0, keepdims=True)


def _top_of_rows(s, sub):
    depth = N_KEYS // SUBLANES
    lv = [s[j * SUBLANES:(j + 1) * SUBLANES] for j in range(depth)]
    ix = [sub + j * SUBLANES for j in range(depth)]
    for phase in range(depth):
        for j in range(phase % 2, depth - 1, 2):
            swap = lv[j + 1] > lv[j]
            lv[j], lv[j + 1] = jnp.where(swap, lv[j + 1], lv[j]), jnp.where(swap, lv[j], lv[j + 1])
            ix[j], ix[j + 1] = jnp.where(swap, ix[j + 1], ix[j]), jnp.where(swap, ix[j], ix[j + 1])
    vals, picked = [], []
    for t in range(TOPK):
        m = _sub_max(lv[0])
        p = _sub_min(jnp.where(lv[0] == m, ix[0], N_KEYS))
        hit = ix[0] == p
        vals.append(m)
        picked.append(p)
        for j in range(depth - 1 - t):
            lv[j] = jnp.where(hit, lv[j + 1], lv[j])
            ix[j] = jnp.where(hit, ix[j + 1], ix[j])
    return vals, picked


def _top_of_sums(v1, i1, v2, i2, sub):
    v1a = jnp.concatenate(v1[:SUBLANES], axis=0)
    v1b = jnp.concatenate(v1[SUBLANES:], axis=0)
    e1a = jnp.concatenate(i1[:SUBLANES], axis=0) * N_KEYS
    e1b = jnp.concatenate(i1[SUBLANES:], axis=0) * N_KEYS
    lv = [jnp.where(sub < TOPK // (b + 1), v1a + v2[b], NEG_INF) for b in range(TOPK)]
    ev = [e1a + i2[b] for b in range(TOPK)]
    lb = v1b + v2[0]
    eb = e1b + i2[0]
    col_a, col_b = sub, sub + SUBLANES
    vals, picked = [], []
    for t in range(TOPK):
        m = jnp.maximum(_sub_max(lv[0]), _sub_max(lb))
        p = jnp.minimum(_sub_min(jnp.where(lv[0] == m, col_a, TOPK)), _sub_min(jnp.where(lb == m, col_b, TOPK)))
        hit_a = col_a == p
        hit_b = col_b == p
        vals.append(m)
        picked.append(jnp.maximum(_sub_max(jnp.where(hit_a, ev[0], -1)), _sub_max(jnp.where(hit_b, eb, -1))))
        for j in range(TOPK - 1 - t):
            lv[j] = jnp.where(hit_a, lv[j + 1], lv[j])
            ev[j] = jnp.where(hit_a, ev[j + 1], ev[j])
        lb = jnp.where(hit_b, NEG_INF, lb)
    return vals, picked


def _route_kernel(x_ref, wq_ref, keys_ref, idx_ref, gate_ref, s_ref, it_ref, gt_ref):
    tm = x_ref.shape[0]
    q = jnp.dot(x_ref[...].astype(jnp.bfloat16), wq_ref[...], preferred_element_type=jnp.float32)
    for j in range(2 * PEER_HEADS):
        s_ref[j] = lax.dot_general(keys_ref[j], q[:, j * D_HALF:(j + 1) * D_HALF].astype(jnp.bfloat16),
                                   (((1,), (1,)), ((), ())), preferred_element_type=jnp.float32)

    sub = lax.broadcasted_iota(jnp.int32, (SUBLANES, ROUTE_W), 0)

    def tile(tl, carry):
        c0 = pl.multiple_of(tl * ROUTE_W, ROUTE_W)

        def head(h, c):
            v1, i1 = _top_of_rows(s_ref[2 * h, :, pl.ds(c0, ROUTE_W)], sub)
            v2, i2 = _top_of_rows(s_ref[2 * h + 1, :, pl.ds(c0, ROUTE_W)], sub)
            tv, te = _top_of_sums(v1, i1, v2, i2, sub)
            tv = jnp.concatenate(tv, axis=0)
            ex = jnp.exp(tv - tv[0:1])
            g = ex / jnp.sum(ex, axis=0, keepdims=True)
            r0 = pl.multiple_of(h * TOPK, TOPK)
            gt_ref[pl.ds(r0, TOPK), :] = g
            it_ref[pl.ds(r0, TOPK), :] = jnp.concatenate(te, axis=0)
            return c

        lax.fori_loop(0, PEER_HEADS, head, 0)
        gate_ref[pl.ds(c0, ROUTE_W), :] = gt_ref[...].T
        idx_ref[pl.ds(c0, ROUTE_W), :] = it_ref[...].T
        return carry

    lax.fori_loop(0, tm // ROUTE_W, tile, 0)


def _peer_route(x1, wq_bf16, keys_bf16, tm=ROUTE_TM):
    t, d = x1.shape
    nq = wq_bf16.shape[1]
    return pl.pallas_call(
        _route_kernel,
        grid=(t // tm,),
        in_specs=[pl.BlockSpec((tm, d), lambda i: (i, 0)),
                  pl.BlockSpec((d, nq), lambda i: (0, 0)),
                  pl.BlockSpec((2 * PEER_HEADS, N_KEYS, D_HALF), lambda i: (0, 0, 0))],
        out_specs=[pl.BlockSpec((tm, PEER_ROWS), lambda i: (i, 0)),
                   pl.BlockSpec((tm, PEER_ROWS), lambda i: (i, 0))],
        out_shape=[jax.ShapeDtypeStruct((t, PEER_ROWS), jnp.int32),
                   jax.ShapeDtypeStruct((t, PEER_ROWS), jnp.float32)],
        scratch_shapes=[pltpu.VMEM((2 * PEER_HEADS, N_KEYS, tm), jnp.float32),
                        pltpu.VMEM((PEER_ROWS, ROUTE_W), jnp.int32),
                        pltpu.VMEM((PEER_ROWS, ROUTE_W), jnp.float32)],
        compiler_params=pltpu.CompilerParams(dimension_semantics=("arbitrary",)),
        name="peer_route",
    )(x1, wq_bf16, keys_bf16)


def _sublane_sums(tiles, sub):
    step = SUBLANES // 2
    while step >= 1:
        low = (sub & step) == 0
        nxt = []
        for j in range(len(tiles) // 2):
            a, b = tiles[j], tiles[j + len(tiles) // 2]
            nxt.append(jnp.where(low, a, pltpu.roll(b, step, axis=0))
                       + jnp.where(low, pltpu.roll(a, SUBLANES - step, axis=0), b))
        tiles = nxt
        step //= 2
    return tiles[0]


def _peer_gather_kernel(idx_ref, x_ref, gate_ref, g2_ref, b2_ref, uv_hbm, o_ref, buf, sem):
    i = pl.program_id(0)
    nb = pl.num_programs(0) - PEER_LAG
    slot = lax.rem(i, PEER_SLOTS)
    prev = lax.rem(i + PEER_SLOTS - PEER_LAG, PEER_SLOTS)

    def start_rows(g, ks):
        for k in ks:
            e = idx_ref[0, 0, g * PEER_ROWS + k]
            pltpu.make_async_copy(uv_hbm.at[e], buf.at[slot, g, k],
                                  sem.at[slot]).start(priority=k % 2)

    groups = PEER_ROWS // SUBLANES

    def tile_of_row(g):
        return jnp.concatenate([x_ref[g:g + 1, c * LANES:(c + 1) * LANES] for c in range(D_TILES)], axis=0)

    def dots(g, x8, sub, fetch):
        parts = []
        for q in range(groups):
            prods = [buf[prev, g, q * SUBLANES + j, 0] * x8 for j in range(SUBLANES)]
            fetch(g, 2 * q)
            parts.append(_sublane_sums(prods, sub))
        return jnp.concatenate(parts, axis=0)

    def combine(g, x8, act, fetch):
        out = None
        for k in range(PEER_ROWS):
            term = buf[prev, g, k, 1] * act[k:k + 1, :]
            out = term if out is None else out + term
            if k % SUBLANES == SUBLANES - 1:
                fetch(g, 2 * (k // SUBLANES) + 1)
        p = ALPHA * x8 + out
        mu = jnp.mean(p, keepdims=True)
        pc = p - mu
        var = jnp.mean(pc * pc, keepdims=True)
        y8 = pc * lax.rsqrt(var + LN_EPS) * g2_ref[...] + b2_ref[...]
        for c in range(D_TILES):
            o_ref[g:g + 1, c * LANES:(c + 1) * LANES] = y8[c:c + 1, :]

    def finish_block(fetch):
        pltpu.make_async_copy(buf.at[prev], buf.at[prev], sem.at[prev]).wait()
        gate_t = gate_ref[...].T
        lane = lax.broadcasted_iota(jnp.int32, gate_t.shape, 1)
        sub = lax.broadcasted_iota(jnp.int32, (SUBLANES, LANES), 0)
        x8 = tile_of_row(0)
        acc = dots(0, x8, sub, fetch)
        for g in range(PEER_G):
            h = jnp.sum(acc, axis=1, keepdims=True)
            gcol = jnp.sum(jnp.where(lane == g, gate_t, 0.0), axis=1, keepdims=True)
            act = jax.nn.gelu(h) * gcol
            x8_cur = x8
            if g + 1 < PEER_G:
                x8 = tile_of_row(g + 1)
                acc = dots(g + 1, x8, sub, fetch)
            combine(g, x8_cur, act, fetch)

    part = SUBLANES // 2

    @pl.when(i < PEER_LAG)
    def _():
        def first(g, c):
            start_rows(g, range(PEER_ROWS))
            return c
        lax.fori_loop(0, PEER_G, first, 0)

    @pl.when(jnp.logical_and(i >= PEER_LAG, i < nb))
    def _():
        finish_block(lambda g, j: start_rows(g, range(j * part, (j + 1) * part)))

    @pl.when(i >= nb)
    def _():
        finish_block(lambda g, j: None)


def _uv_pack_kernel(u_ref, v_ref, o_ref):
    rows = u_ref.shape[0]
    o_ref[:, 0] = u_ref[...].reshape(rows, D_TILES, LANES)
    o_ref[:, 1] = v_ref[...].reshape(rows, D_TILES, LANES)


def _uv_pack(u_tab, v_tab, rows=UV_PACK_ROWS):
    n_exp, d = u_tab.shape
    return pl.pallas_call(
        _uv_pack_kernel,
        grid=(n_exp // rows,),
        in_specs=[pl.BlockSpec((rows, d), lambda i: (i, 0)), pl.BlockSpec((rows, d), lambda i: (i, 0))],
        out_specs=pl.BlockSpec((rows, 2, D_TILES, LANES), lambda i: (i, 0, 0, 0)),
        out_shape=jax.ShapeDtypeStruct((n_exp, 2, D_TILES, LANES), jnp.float32),
        name="uv_pack",
    )(u_tab, v_tab)


def _peer_gather(x1, idx, gate, u_tab, v_tab, ln_g, ln_b):
    t, d = x1.shape
    nb = t // PEER_G
    uv = _uv_pack(u_tab, v_tab)
    idx3 = idx.reshape(nb, 1, PEER_G * PEER_ROWS)
    cur = lambda i: (jnp.minimum(i, nb - 1), 0, 0)
    prv2 = lambda i: (jnp.maximum(i - PEER_LAG, 0), 0)
    return pl.pallas_call(
        _peer_gather_kernel,
        grid=(nb + PEER_LAG,),
        in_specs=[
            pl.BlockSpec((1, 1, PEER_G * PEER_ROWS), cur, memory_space=pltpu.SMEM),
            pl.BlockSpec((PEER_G, d), prv2),
            pl.BlockSpec((PEER_G, PEER_ROWS), prv2),
            pl.BlockSpec((D_TILES, LANES), lambda i: (0, 0)),
            pl.BlockSpec((D_TILES, LANES), lambda i: (0, 0)),
            pl.BlockSpec(memory_space=pl.ANY),
        ],
        out_specs=pl.BlockSpec((PEER_G, d), prv2),
        out_shape=jax.ShapeDtypeStruct((t, d), jnp.float32),
        scratch_shapes=[
            pltpu.VMEM((PEER_SLOTS, PEER_G, PEER_ROWS, 2, D_TILES, LANES), jnp.float32),
            pltpu.SemaphoreType.DMA((PEER_SLOTS,)),
        ],
        compiler_params=pltpu.CompilerParams(dimension_semantics=("arbitrary",),
                                             vmem_limit_bytes=PEER_VMEM_BYTES),
        name="peer_gather",
    )(idx3, x1, gate, ln_g.reshape(D_TILES, LANES), ln_b.reshape(D_TILES, LANES), uv)


def kernel(x, w_in, b_gate, conv_w, conv_b, rg_w_a, rg_b_a, rg_w_x, rg_b_x, rg_lambda, gm_ln_g, gm_ln_b, gm_ws, gm_bs, w_out, ln1_g, ln1_b, peer_wq, peer_sub_keys, peer_u, peer_v, ln2_g, ln2_b):
    bsz, s, d = x.shape
    assert w_in.shape[0] == 1, "single-layer stack"
    l = 0
    x1 = _token_mix(x, w_in[l], b_gate[l], conv_w[l], conv_b[l], rg_w_a[l], rg_b_a[l], rg_w_x[l], rg_b_x[l],
                    rg_lambda[l], gm_ln_g[l], gm_ln_b[l], gm_ws[l], gm_bs[l], w_out[l], ln1_g[l], ln1_b[l])
    keys = peer_sub_keys[l].reshape(2 * PEER_HEADS, N_KEYS, D_HALF).astype(jnp.bfloat16)
    idx, gate = _peer_route(x1, peer_wq[l].astype(jnp.bfloat16), keys)
    x2 = _peer_gather(x1, idx, gate, peer_u[l], peer_v[l], ln2_g[l], ln2_b[l])
    return x2.reshape(bsz, s, d)
```
